```python
import jax, jax.numpy as jnp
from jax import lax
import numpy as np

D_MODEL = 4096
BATCH = 2
SEQ = 8192
DEPTH = 4

HG_HEADS = 8
HG_DK = 128
HG_DV = 128
HG_CHUNK = 64
FOX_HEADS = 8
FOX_DIM = 128
FOX_F_BIAS_MEAN = 2.0
MLA_HEADS = 16
MLA_Q_LORA = 768
MLA_KV_LORA = 512
MLA_NOPE = 128
MLA_ROPE = 64
MLA_V = 128
ROPE_THETA = 10000.0
MIX_WIDTH = HG_HEADS * HG_DV + FOX_HEADS * FOX_DIM + MLA_HEADS * MLA_V
MEM_TOKENS = 256
MEM_HEADS = 4
MEM_DIM = 128
D_FF = 4 * D_MODEL
Q_BLOCK = 128
EPS = 1e-6
MASK_VALUE = -1e30
LB_FLOOR = 1e-30
IN_SIZES = (HG_HEADS * HG_DK, HG_HEADS * HG_DK, HG_HEADS * HG_DV, HG_HEADS * HG_DV,
            FOX_HEADS * FOX_DIM, FOX_HEADS * FOX_DIM, FOX_HEADS * FOX_DIM, FOX_HEADS,
            MLA_Q_LORA, MLA_KV_LORA, MLA_ROPE)
IN_COLS = 4 * 1024 + 3 * 1024 + 8 + 768 + 512 + 64

kernel_name = "hybrid_hgrn2_fox_mla_trunk"


def rms_norm(x, g):
    xf = x.astype(jnp.float32)
    y = xf * lax.rsqrt(jnp.mean(xf * xf, axis=-1, keepdims=True) + EPS)
    return (y * g.astype(jnp.float32)).astype(x.dtype)


def head_rms_norm(o, g):
    return rms_norm(o, g.reshape(o.shape[-2], o.shape[-1]))


def rope(x, positions):
    half = x.shape[-1] // 2
    inv = ROPE_THETA ** (-jnp.arange(half, dtype=jnp.float32) / half)
    ang = positions.astype(jnp.float32)[..., None] * inv
    cos = jnp.cos(ang)[:, :, None, :]
    sin = jnp.sin(ang)[:, :, None, :]
    xf = x.astype(jnp.float32)
    x1, x2 = xf[..., :half], xf[..., half:]
    return jnp.concatenate([x1 * cos - x2 * sin, x1 * sin + x2 * cos], axis=-1).astype(x.dtype)


def hgrn2_chunked(q, k, v, log_f):
    b_, s_, h_, dk = q.shape
    dv = v.shape[-1]
    n = s_ // HG_CHUNK

    def chunks(t):
        return t.astype(jnp.float32).reshape(b_, n, HG_CHUNK, h_, t.shape[-1]).transpose(1, 0, 3, 2, 4)

    qc, kc, vc = chunks(q), chunks(k), chunks(v)
    bc = jnp.cumsum(chunks(log_f), axis=3)
    tri = jnp.tril(jnp.ones((HG_CHUNK, HG_CHUNK), dtype=bool))[:, :, None]

    def step(state, xs):
        qi, ki, vi, bi = xs
        rel = bi[:, :, :, None, :] - bi[:, :, None, :, :]
        decay = jnp.where(tri, jnp.exp(jnp.minimum(rel, 0.0)), 0.0)
        scores = jnp.einsum('bhtk,bhsk,bhtsk->bhts', qi, ki, decay)
        o = (jnp.einsum('bhts,bhsv->bhtv', scores, vi)
             + jnp.einsum('bhtk,bhkv->bhtv', qi * jnp.exp(bi), state))
        b_end = bi[:, :, -1:, :]
        state = (jnp.exp(b_end[:, :, 0, :, None]) * state
                 + jnp.einsum('bhsk,bhsv->bhkv', ki * jnp.exp(b_end - bi), vi))
        return state, o

    state0 = jnp.zeros((b_, h_, dk, dv), jnp.float32)
    _, ys = lax.scan(step, state0, (qc, kc, vc, bc))
    return ys.transpose(1, 0, 3, 2, 4).reshape(b_, s_, h_, dv)


def causal_block_attention(q, k, v, scale, log_f=None):
    b_, s_, h_, _ = q.shape
    dv = v.shape[-1]
    n_blocks = s_ // Q_BLOCK
    k_idx = jnp.arange(s_)
    cum = None if log_f is None else jnp.cumsum(log_f.astype(jnp.float32), axis=1).transpose(0, 2, 1)

    def block(i):
        start = i * Q_BLOCK
        qb = lax.dynamic_slice_in_dim(q, start, Q_BLOCK, axis=1)
        logits = jnp.einsum('bthd,bshd->bhts', qb, k).astype(jnp.float32) * scale
        if cum is not None:
            cb = lax.dynamic_slice_in_dim(cum, start, Q_BLOCK, axis=2)
            logits = logits + (cb[..., :, None] - cum[..., None, :])
        q_idx = start + jnp.arange(Q_BLOCK)
        logits = jnp.where(k_idx[None, :] <= q_idx[:, None], logits, MASK_VALUE)
        p = jax.nn.softmax(logits, axis=-1).astype(v.dtype)
        return jnp.einsum('bhts,bshd->bthd', p, v)

    out = lax.map(block, jnp.arange(n_blocks))
    return out.transpose(1, 0, 2, 3, 4).reshape(b_, s_, h_, dv)


def memory_cross_attention(hn, mem_n, w_q, w_k, w_v, w_out):
    b_, s_, _ = hn.shape
    m_ = mem_n.shape[1]
    q = (hn @ w_q).reshape(b_, s_, MEM_HEADS, MEM_DIM)
    k = (mem_n @ w_k).reshape(b_, m_, MEM_HEADS, MEM_DIM)
    v = (mem_n @ w_v).reshape(b_, m_, MEM_HEADS, MEM_DIM)
    logits = jnp.einsum('bthd,bmhd->bhtm', q, k).astype(jnp.float32) * (MEM_DIM ** -0.5)
    p = jax.nn.softmax(logits, axis=-1).astype(v.dtype)
    o = jnp.einsum('bhtm,bmhd->bthd', p, v).reshape(b_, s_, MEM_HEADS * MEM_DIM)
    return o @ w_out


def setup_inputs(seed: int = 0) -> dict:
    key = jax.random.key(seed)
    ks = jax.random.split(key, 25)
    f32 = jnp.float32

    def nrm(k, shape, scale):
        return jax.random.normal(k, shape, f32) * scale

    def gain(k, shape):
        return 1.0 + 0.02 * jax.random.normal(k, shape, f32)

    x = jax.random.normal(ks[0], (BATCH, SEQ, D_MODEL), f32)
    mem = jax.random.normal(ks[1], (BATCH, MEM_TOKENS, D_MODEL), f32)
    offsets = jax.random.randint(ks[2], (BATCH, 1), 0, 4096, dtype=jnp.int32)
    positions = (jnp.arange(SEQ, dtype=jnp.int32)[None, :] + offsets).astype(jnp.int32)
    return {
        "x": x,
        "mem": mem,
        "positions": positions,
        "w_in": nrm(ks[3], (DEPTH, D_MODEL, IN_COLS), D_MODEL ** -0.5),
        "hg_lb_logits": nrm(ks[4], (DEPTH, HG_HEADS * HG_DK), 0.1),
        "fox_f_bias": FOX_F_BIAS_MEAN + nrm(ks[5], (DEPTH, FOX_HEADS), 0.5),
        "mla_q_norm_g": gain(ks[6], (DEPTH, MLA_Q_LORA)),
        "mla_kv_norm_g": gain(ks[7], (DEPTH, MLA_KV_LORA)),
        "w_uq": nrm(ks[8], (DEPTH, MLA_Q_LORA, MLA_HEADS * (MLA_NOPE + MLA_ROPE)), MLA_Q_LORA ** -0.5),
        "w_ukv": nrm(ks[9], (DEPTH, MLA_KV_LORA, MLA_HEADS * (MLA_NOPE + MLA_V)), MLA_KV_LORA ** -0.5),
        "mix_out_g": gain(ks[10], (DEPTH, MIX_WIDTH)),
        "w_o": nrm(ks[11], (DEPTH, MIX_WIDTH, D_MODEL), MIX_WIDTH ** -0.5),
        "mem_norm_g": gain(ks[12], (DEPTH, D_MODEL)),
        "w_mq": nrm(ks[13], (DEPTH, D_MODEL, MEM_HEADS * MEM_DIM), D_MODEL ** -0.5),
        "w_mk": nrm(ks[14], (DEPTH, D_MODEL, MEM_HEADS * MEM_DIM), D_MODEL ** -0.5),
        "w_mv": nrm(ks[15], (DEPTH, D_MODEL, MEM_HEADS * MEM_DIM), D_MODEL ** -0.5),
        "w_mo": nrm(ks[16], (DEPTH, MEM_HEADS * MEM_DIM, D_MODEL), (MEM_HEADS * MEM_DIM) ** -0.5),
        "w_ff1": nrm(ks[17], (DEPTH, D_MODEL, D_FF), D_MODEL ** -0.5),
        "w_ff2": nrm(ks[18], (DEPTH, D_FF, D_MODEL), D_FF ** -0.5),
        "pre_mix_g": gain(ks[19], (DEPTH, D_MODEL)),
        "post_mix_g": gain(ks[20], (DEPTH, D_MODEL)),
        "pre_mem_g": gain(ks[21], (DEPTH, D_MODEL)),
        "post_mem_g": gain(ks[22], (DEPTH, D_MODEL)),
        "pre_ffn_g": gain(ks[23], (DEPTH, D_MODEL)),
        "post_ffn_g": gain(ks[24], (DEPTH, D_MODEL)),
    }


def reference(x, mem, positions, w_in, hg_lb_logits, fox_f_bias, mla_q_norm_g, mla_kv_norm_g,
              w_uq, w_ukv, mix_out_g, w_o, mem_norm_g, w_mq, w_mk, w_mv, w_mo, w_ff1, w_ff2,
              pre_mix_g, post_mix_g, pre_mem_g, post_mem_g, pre_ffn_g, post_ffn_g):
    b_, s_, _ = x.shape
    split_points = tuple(int(c) for c in np.cumsum(IN_SIZES)[:-1])
    group_points = (HG_HEADS * HG_DV, HG_HEADS * HG_DV + FOX_HEADS * FOX_DIM)

    p_lb = jax.nn.softmax(hg_lb_logits.astype(jnp.float32), axis=0)
    lower_bounds = jnp.cumsum(p_lb, axis=0) - p_lb[0:1]

    h = x
    for l in range(DEPTH):
        hn = rms_norm(h, pre_mix_g[l])
        proj = hn @ w_in[l]
        (hq, hf, hi, hg, fq, fk, fv, ff, cq, ckv, kr) = jnp.split(proj, split_points, axis=-1)
        g_a, g_b, g_c = jnp.split(mix_out_g[l], group_points)

        lb = jnp.clip(lower_bounds[l].reshape(HG_HEADS, HG_DK), 0.0, 1.0 - 1e-6)
        log_lb = jnp.log(jnp.maximum(lb, LB_FLOOR))
        log_one_minus_lb = jnp.log1p(-lb)
        z = hf.astype(jnp.float32).reshape(b_, s_, HG_HEADS, HG_DK)
        log_f_a = jnp.logaddexp(log_lb, log_one_minus_lb + jax.nn.log_sigmoid(z))
        k_a = jnp.exp(log_one_minus_lb + jax.nn.log_sigmoid(-z))
        o_a = hgrn2_chunked(hq.reshape(b_, s_, HG_HEADS, HG_DK), k_a,
                            hi.reshape(b_, s_, HG_HEADS, HG_DV), log_f_a)
        o_a = head_rms_norm(o_a, g_a) * jax.nn.silu(hg.astype(jnp.float32).reshape(b_, s_, HG_HEADS, HG_DV))

        log_f_b = jax.nn.log_sigmoid(ff.astype(jnp.float32) + fox_f_bias[l].astype(jnp.float32))
        o_b = causal_block_attention(fq.reshape(b_, s_, FOX_HEADS, FOX_DIM),
                                     fk.reshape(b_, s_, FOX_HEADS, FOX_DIM),
                                     fv.reshape(b_, s_, FOX_HEADS, FOX_DIM),
                                     FOX_DIM ** -0.5, log_f_b)
        o_b = head_rms_norm(o_b, g_b)

        q_full = (rms_norm(cq, mla_q_norm_g[l]) @ w_uq[l]).reshape(b_, s_, MLA_HEADS, MLA_NOPE + MLA_ROPE)
        kv_full = (rms_norm(ckv, mla_kv_norm_g[l]) @ w_ukv[l]).reshape(b_, s_, MLA_HEADS, MLA_NOPE + MLA_V)
        q_nope, q_rot = jnp.split(q_full, (MLA_NOPE,), axis=-1)
        k_nope, v_c = jnp.split(kv_full, (MLA_NOPE,), axis=-1)
        k_rot = jnp.broadcast_to(rope(kr[:, :, None, :], positions), (b_, s_, MLA_HEADS, MLA_ROPE))
        q_c = jnp.concatenate([q_nope, rope(q_rot, positions)], axis=-1)
        k_c = jnp.concatenate([k_nope, k_rot.astype(k_nope.dtype)], axis=-1)
        o_c = causal_block_attention(q_c, k_c, v_c, (MLA_NOPE + MLA_ROPE) ** -0.5)
        o_c = head_rms_norm(o_c, g_c)

        mix = jnp.concatenate([o_a.reshape(b_, s_, -1), o_b.reshape(b_, s_, -1).astype(jnp.float32),
                               o_c.reshape(b_, s_, -1).astype(jnp.float32)], axis=-1).astype(h.dtype)
        h = h + rms_norm(mix @ w_o[l], post_mix_g[l])

        hn = rms_norm(h, pre_mem_g[l])
        mem_n = rms_norm(mem, mem_norm_g[l])
        y = memory_cross_attention(hn, mem_n, w_mq[l], w_mk[l], w_mv[l], w_mo[l])
        h = h + rms_norm(y, post_mem_g[l])

        hn = rms_norm(h, pre_ffn_g[l])
        y = jnp.square(jax.nn.relu(hn @ w_ff1[l])) @ w_ff2[l]
        h = h + rms_norm(y, post_ffn_g[l])
    return h
```

```python
import functools

import jax
import jax.numpy as jnp
from jax import lax
from jax.experimental import pallas as pl
from jax.experimental.pallas import tpu as pltpu

F32 = jnp.float32
BF16 = jnp.bfloat16

D_MODEL = 4096
DEPTH = 4
HG_HEADS = 8
HG_DK = 128
HG_DV = 128
FOX_HEADS = 8
FOX_DIM = 128
MLA_HEADS = 16
MLA_Q_LORA = 768
MLA_KV_LORA = 512
MLA_NOPE = 128
MLA_ROPE = 64
MLA_V = 128
ROPE_THETA = 10000.0
MEM_HEADS = 4
MEM_DIM = 128
EPS = 1e-6
MASK_VALUE = -1e30
LB_FLOOR = 1e-30

LANES = 128
SUBLANES = 8
HG_CHUNK = 64
MLA_QK = 2 * LANES

P16_HQ, P16_HI, P16_HG, P16_FQ, P16_FK, P16_FV = 0, 8, 16, 24, 32, 40
P16_COLS = 6 * 1024
P32_HF = 4
P32_MISC = 18
P32_COLS = 20 * LANES
FF_LANE = 64


def _cparams(sem, vmem_mb=None):
    kw = dict(dimension_semantics=sem)
    if vmem_mb is not None:
        kw["vmem_limit_bytes"] = vmem_mb * 1024 * 1024
    return pltpu.CompilerParams(**kw)


def _mm_kernel(a_ref, b_ref, o_ref, *scratch, nk, relu2):
    def finish(r):
        if relu2:
            r = jnp.square(jnp.maximum(r, 0.0))
        o_ref[...] = r.astype(o_ref.dtype)

    if nk == 1:
        finish(jnp.dot(a_ref[...], b_ref[...], preferred_element_type=F32))
        return
    (acc_ref,) = scratch
    k = pl.program_id(2)

    @pl.when(k == 0)
    def _():
        acc_ref[...] = jnp.zeros_like(acc_ref)

    acc_ref[...] += jnp.dot(a_ref[...], b_ref[...], preferred_element_type=F32)

    @pl.when(k == nk - 1)
    def _():
        finish(acc_ref[...])


def _matmul(a, w, layer, out_dtype, *, tm, tn, tk=None, relu2=False, name="mm"):
    m, kdim = a.shape
    n = w.shape[-1]
    tk = kdim if tk is None else min(tk, kdim)
    tm, tn = min(tm, m), min(tn, n)
    assert m % tm == 0 and n % tn == 0 and kdim % tk == 0
    nk = kdim // tk
    scratch = [] if nk == 1 else [pltpu.VMEM((tm, tn), F32)]
    return pl.pallas_call(
        functools.partial(_mm_kernel, nk=nk, relu2=relu2),
        grid=(m // tm, n // tn, nk),
        in_specs=[
            pl.BlockSpec((tm, tk), lambda i, j, k: (i, k)),
            pl.BlockSpec((None, tk, tn), lambda i, j, k: (layer, k, j)),
        ],
        out_specs=pl.BlockSpec((tm, tn), lambda i, j, k: (i, j)),
        out_shape=jax.ShapeDtypeStruct((m, n), out_dtype),
        scratch_shapes=scratch,
        compiler_params=_cparams(("parallel", "parallel", "arbitrary"), 56),
        name=name,
    )(a, w)


def _rms(x, g):
    return x * lax.rsqrt(jnp.mean(x * x, axis=-1, keepdims=True) + EPS) * g


def _rms_cast_kernel(x_ref, g_ref, o_ref):
    o_ref[...] = _rms(x_ref[...].astype(F32), g_ref[...]).astype(o_ref.dtype)


def _rms_cast(x, g, layer, *, tm=256, name="rms_cast"):
    m, d = x.shape
    tm = min(tm, m)
    return pl.pallas_call(
        _rms_cast_kernel,
        grid=(m // tm,),
        in_specs=[pl.BlockSpec((tm, d), lambda i: (i, 0)),
                  pl.BlockSpec((None, 1, d), lambda i: (layer, 0, 0))],
        out_specs=pl.BlockSpec((tm, d), lambda i: (i, 0)),
        out_shape=jax.ShapeDtypeStruct((m, d), BF16),
        compiler_params=_cparams(("parallel",)),
        name=name,
    )(x, g)


def _resid_norm_kernel(h_ref, y_ref, gp_ref, *rest, with_next):
    h_new = h_ref[...] + _rms(y_ref[...], gp_ref[...])
    if with_next:
        gn_ref, ho_ref, hn_ref = rest
        hn_ref[...] = _rms(h_new, gn_ref[...]).astype(hn_ref.dtype)
    else:
        (ho_ref,) = rest
    ho_ref[...] = h_new


def _resid_norm(h, y, g_post, layer, g_next=None, layer_next=None, *, tm=256):
    m, d = h.shape
    tm = min(tm, m)
    row = pl.BlockSpec((tm, d), lambda i: (i, 0))
    with_next = g_next is not None
    in_specs = [row, row, pl.BlockSpec((None, 1, d), lambda i: (layer, 0, 0))]
    args = [h, y, g_post]
    out_shape = [jax.ShapeDtypeStruct((m, d), F32)]
    out_specs = [row]
    if with_next:
        in_specs.append(pl.BlockSpec((None, 1, d), lambda i: (layer_next, 0, 0)))
        args.append(g_next)
        out_shape.append(jax.ShapeDtypeStruct((m, d), BF16))
        out_specs.append(row)
    res = pl.pallas_call(
        functools.partial(_resid_norm_kernel, with_next=with_next),
        grid=(m // tm,),
        in_specs=in_specs,
        out_specs=out_specs,
        out_shape=out_shape,
        compiler_params=_cparams(("parallel",)),
        name="resid_norm",
    )(*args)
    return (res[0], res[1]) if with_next else (res[0], None)


def _lower_bound_kernel(x_ref, llb_ref, l1m_ref):
    depth = x_ref.shape[0]
    rows = [x_ref[i:i + 1, :].astype(F32) for i in range(depth)]
    mx = functools.reduce(jnp.maximum, rows)
    ex = [jnp.exp(r - mx) for r in rows]
    tot = functools.reduce(lambda a, b: a + b, ex)
    p = [e / tot for e in ex]
    cum = p[0]
    for i in range(depth):
        if i > 0:
            cum = cum + p[i]
        lb = jnp.clip(cum - p[0], 0.0, 1.0 - 1e-6)
        llb_ref[i:i + 1, :] = jnp.log(jnp.maximum(lb, LB_FLOOR))
        l1m_ref[i:i + 1, :] = jnp.log1p(-lb)


def _lower_bounds(logits):
    shp = jax.ShapeDtypeStruct(logits.shape, F32)
    return pl.pallas_call(_lower_bound_kernel, out_shape=[shp, shp], name="hgrn_lower_bounds")(logits)


def _hgrn_kernel(q_ref, v_ref, gate_ref, z_ref, llb_ref, l1m_ref, g_ref, o_ref,
                 state_ref, b_scr, k_scr, v_scr, *, n_chunks):
    c_len = HG_CHUNK
    n_rows = c_len // SUBLANES

    @pl.when(pl.program_id(2) == 0)
    def _():
        state_ref[...] = jnp.zeros_like(state_ref)

    llb = llb_ref[...]
    l1m = l1m_ref[...]
    gain = g_ref[...]
    r_i = lax.broadcasted_iota(jnp.int32, (c_len, c_len), 0)
    c_i = lax.broadcasted_iota(jnp.int32, (c_len, c_len), 1)
    tri = (c_i <= r_i).astype(F32)
    sub = lax.broadcasted_iota(jnp.int32, (SUBLANES, LANES), 0)

    def chunk(c, carry):
        rows = pl.ds(pl.multiple_of(c * c_len, c_len), c_len)
        z = z_ref[rows, :]
        q = q_ref[rows, :].astype(F32)
        v16 = v_ref[rows, :]
        sp = jnp.log1p(jnp.exp(-jnp.abs(z)))
        ls = jnp.minimum(z, 0.0) - sp
        a2 = l1m + ls
        log_f = jnp.maximum(llb, a2) + jnp.log1p(jnp.exp(-jnp.abs(llb - a2)))
        kk = jnp.exp(l1m + (ls - z))
        b = jnp.dot(tri, log_f, preferred_element_type=F32, precision=lax.Precision.HIGHEST)
        b_scr[...] = b
        k_scr[...] = kk
        v_scr[...] = v16.astype(F32)

        q_rows = [q[r * SUBLANES:(r + 1) * SUBLANES, :] for r in range(n_rows)]
        b_rows = [b[r * SUBLANES:(r + 1) * SUBLANES, :] for r in range(n_rows)]
        o_rows = [jnp.zeros((SUBLANES, LANES), F32) for _ in range(n_rows)]
        for g in range(n_rows):
            for i in range(SUBLANES):
                s = g * SUBLANES + i
                bs = b_scr[s:s + 1, :]
                ks = k_scr[s:s + 1, :]
                vs = v_scr[s:s + 1, :]
                for r in range(g, n_rows):
                    e = jnp.exp(jnp.minimum(b_rows[r] - bs, 0.0))
                    if r == g and i > 0:
                        e = jnp.where(sub >= i, e, 0.0)
                    p = e * (q_rows[r] * ks)
                    o_rows[r] = o_rows[r] + jnp.sum(p, axis=-1, keepdims=True) * vs
        o = jnp.concatenate(o_rows, axis=0)

        state_t = state_ref[...]
        qe = (q * jnp.exp(b)).astype(BF16)
        o = o + lax.dot_general(qe, state_t.astype(BF16), (((1,), (1,)), ((), ())),
                                preferred_element_type=F32)
        b_end = b[c_len - 1:c_len, :]
        kd = (kk * jnp.exp(b_end - b)).astype(BF16)
        upd = lax.dot_general(v16, kd, (((0,), (0,)), ((), ())), preferred_element_type=F32)
        state_ref[...] = state_t * jnp.exp(b_end) + upd

        gate = gate_ref[rows, :].astype(F32)
        y = _rms(o, gain) * (gate * jax.nn.sigmoid(gate))
        o_ref[rows, :] = y.astype(o_ref.dtype)
        return carry

    lax.fori_loop(0, n_chunks, chunk, 0)


def _hgrn(proj16, proj32, llb, l1m, mix_g, layer, *, batch, seq, blk=512):
    blk = min(blk, seq)
    ns = seq // blk
    tok = lambda col0: pl.BlockSpec((blk, LANES), lambda b, h, s: (b * ns + s, col0 + h))
    par = pl.BlockSpec((None, 1, LANES), lambda b, h, s: (layer, 0, h))
    return pl.pallas_call(
        functools.partial(_hgrn_kernel, n_chunks=blk // HG_CHUNK),
        grid=(batch, HG_HEADS, ns),
        in_specs=[tok(P16_HQ), tok(P16_HI), tok(P16_HG), tok(P32_HF), par, par, par],
        out_specs=pl.BlockSpec((blk, LANES), lambda b, h, s: (b * ns + s, h)),
        out_shape=jax.ShapeDtypeStruct((batch * seq, HG_HEADS * HG_DV), BF16),
        scratch_shapes=[pltpu.VMEM((HG_DV, HG_DK), F32)] + [pltpu.VMEM((HG_CHUNK, LANES), F32)] * 3,
        compiler_params=_cparams(("parallel", "parallel", "arbitrary")),
        name="hgrn2_scan",
    )(proj16, proj16, proj16, proj32, llb, l1m, mix_g)


def _fox_cum_kernel(x_ref, bias_ref, col_ref, row_ref, carry_ref, *, blk):
    @pl.when(pl.program_id(1) == 0)
    def _():
        carry_ref[...] = jnp.zeros_like(carry_ref)

    x = x_ref[...] + bias_ref[...]
    lf = jnp.minimum(x, 0.0) - jnp.log1p(jnp.exp(-jnp.abs(x)))
    r_i = lax.broadcasted_iota(jnp.int32, (blk, blk), 0)
    c_i = lax.broadcasted_iota(jnp.int32, (blk, blk), 1)
    tri = (c_i <= r_i).astype(F32)
    cum = jnp.dot(tri, lf, preferred_element_type=F32, precision=lax.Precision.HIGHEST) + carry_ref[...]
    carry_ref[...] = cum[blk - 1:blk, :]
    cum_t = cum.T
    for h in range(FOX_HEADS):
        lane = FF_LANE + h
        col_ref[h] = jnp.broadcast_to(cum[:, lane:lane + 1], (blk, LANES))
        row_ref[h] = cum_t[lane:lane + 1, :]


def _fox_cum(proj32, bias_row, layer, *, batch, seq, blk=256):
    blk = min(blk, seq)
    nb = seq // blk
    t = batch * seq
    return pl.pallas_call(
        functools.partial(_fox_cum_kernel, blk=blk),
        grid=(batch, nb),
        in_specs=[pl.BlockSpec((blk, LANES), lambda b, j: (b * nb + j, P32_MISC)),
                  pl.BlockSpec((None, 1, LANES), lambda b, j: (layer, 0, 0))],
        out_specs=[pl.BlockSpec((FOX_HEADS, blk, LANES), lambda b, j: (0, b * nb + j, 0)),
                   pl.BlockSpec((FOX_HEADS, 1, blk), lambda b, j: (0, 0, b * nb + j))],
        out_shape=[jax.ShapeDtypeStruct((FOX_HEADS, t, LANES), F32),
                   jax.ShapeDtypeStruct((FOX_HEADS, 1, t), F32)],
        scratch_shapes=[pltpu.VMEM((1, LANES), F32)],
        compiler_params=_cparams(("parallel", "arbitrary")),
        name="fox_cumgate",
    )(proj32, bias_row)


def _flash_kernel(*refs, scale, tq, tk, has_bias):
    if has_bias:
        q_ref, k_ref, v_ref, cq_ref, ck_ref, g_ref, o_ref, m_scr, l_scr, acc_scr = refs
    else:
        q_ref, k_ref, v_ref, g_ref, o_ref, m_scr, l_scr, acc_scr = refs
    i = pl.program_id(2)
    j = pl.program_id(3)
    j_last = ((i + 1) * tq - 1) // tk

    @pl.when(j == 0)
    def _():
        m_scr[...] = jnp.full_like(m_scr, MASK_VALUE)
        l_scr[...] = jnp.zeros_like(l_scr)
        acc_scr[...] = jnp.zeros_like(acc_scr)

    @pl.when(j <= j_last)
    def _():
        s = lax.dot_general(q_ref[...], k_ref[...], (((1,), (1,)), ((), ())),
                            preferred_element_type=F32) * scale
        if has_bias:
            cq = cq_ref[...]
            s = s + (jnp.concatenate([cq] * (tk // LANES), axis=1) - ck_ref[...])
        row = i * tq + lax.broadcasted_iota(jnp.int32, (tq, tk), 0)
        col = j * tk + lax.broadcasted_iota(jnp.int32, (tq, tk), 1)
        s = jnp.where(col <= row, s, MASK_VALUE)
        m_prev = m_scr[...]
        m_new = jnp.maximum(m_prev, jnp.max(s, axis=-1, keepdims=True))
        alpha = jnp.exp(m_prev - m_new)
        p = jnp.exp(s - m_new[:, :1])
        l_scr[...] = alpha * l_scr[...] + jnp.sum(p, axis=-1, keepdims=True)
        acc_scr[...] = alpha * acc_scr[...] + jnp.dot(p.astype(BF16), v_ref[...],
                                                      preferred_element_type=F32)
        m_scr[...] = m_new

    @pl.when(j == j_last)
    def _():
        o = acc_scr[...] / l_scr[...]
        o_ref[...] = _rms(o, g_ref[...]).astype(o_ref.dtype)


def _flash(q, k, v, gain, layer, *, q_map, kv_map, g_map, heads, dk, bias=None,
           batch, seq, scale, tq=512, tk=512, name):
    tq, tk = min(tq, seq), min(tk, seq)
    nq, nkv = seq // tq, seq // tk
    dv = LANES

    def kv_row(b, i, j):
        return b * nkv + jnp.minimum(j, ((i + 1) * tq - 1) // tk)

    q_blk = (tq, dk) if q.ndim == 2 else (None, tq, dk)
    k_blk = (tk, dk) if k.ndim == 2 else (None, tk, dk)
    v_blk = (tk, dv) if v.ndim == 2 else (None, tk, dv)
    in_specs = [
        pl.BlockSpec(q_blk, lambda b, h, i, j: q_map(h, b * nq + i)),
        pl.BlockSpec(k_blk, lambda b, h, i, j: kv_map[0](h, kv_row(b, i, j))),
        pl.BlockSpec(v_blk, lambda b, h, i, j: kv_map[1](h, kv_row(b, i, j))),
    ]
    args = [q, k, v]
    if bias is not None:
        c_col, c_row = bias
        in_specs += [pl.BlockSpec((None, tq, LANES), lambda b, h, i, j: (h, b * nq + i, 0)),
                     pl.BlockSpec((None, 1, tk), lambda b, h, i, j: (h, 0, kv_row(b, i, j)))]
        args += [c_col, c_row]
    in_specs.append(pl.BlockSpec((None, 1, LANES), lambda b, h, i, j: (layer, 0, g_map(h))))
    args.append(gain)
    return pl.pallas_call(
        functools.partial(_flash_kernel, scale=scale, tq=tq, tk=tk, has_bias=bias is not None),
        grid=(batch, heads, nq, nkv),
        in_specs=in_specs,
        out_specs=pl.BlockSpec((tq, dv), lambda b, h, i, j: (b * nq + i, h)),
        out_shape=jax.ShapeDtypeStruct((batch * seq, heads * dv), BF16),
        scratch_shapes=[pltpu.VMEM((tq, LANES), F32), pltpu.VMEM((tq, LANES), F32),
                        pltpu.VMEM((tq, dv), F32)],
        compiler_params=_cparams(("parallel", "parallel", "parallel", "arbitrary")),
        name=name,
    )(*args)


def _rope128(x, tab):
    c, s1, s2 = tab[:, :LANES], tab[:, LANES:2 * LANES], tab[:, 2 * LANES:]
    half = MLA_ROPE // 2
    return x * c + pltpu.roll(x, LANES - half, axis=1) * s1 + pltpu.roll(x, half, axis=1) * s2


def _mla_prep_kernel(ckv_ref, cq_ref, misc_ref, ang_ref, gq_ref, gkv_ref,
                     cqn_ref, ckvn_ref, krot_ref, tab_ref):
    cqn_ref[...] = _rms(cq_ref[...], gq_ref[...]).astype(cqn_ref.dtype)
    ckvn_ref[...] = _rms(ckv_ref[...], gkv_ref[...]).astype(ckvn_ref.dtype)
    ang = ang_ref[...]
    cos, sin = jnp.cos(ang), jnp.sin(ang)
    lane = lax.broadcasted_iota(jnp.int32, ang.shape, 1)
    half = MLA_ROPE // 2
    tab = jnp.concatenate([
        jnp.where(lane < MLA_ROPE, cos, 0.0),
        jnp.where(lane < half, -sin, 0.0),
        jnp.where((lane >= half) & (lane < MLA_ROPE), sin, 0.0)], axis=1)
    tab_ref[...] = tab
    krot_ref[...] = _rope128(misc_ref[...], tab).astype(krot_ref.dtype)


def _mla_prep(proj32, ang, gq, gkv, layer, *, tm=512):
    t = proj32.shape[0]
    tm = min(tm, t)
    return pl.pallas_call(
        _mla_prep_kernel,
        grid=(t // tm,),
        in_specs=[pl.BlockSpec((tm, MLA_KV_LORA), lambda i: (i, 0)),
                  pl.BlockSpec((tm, MLA_Q_LORA), lambda i: (i, 2)),
                  pl.BlockSpec((tm, LANES), lambda i: (i, P32_MISC)),
                  pl.BlockSpec((tm, LANES), lambda i: (i, 0)),
                  pl.BlockSpec((None, 1, MLA_Q_LORA), lambda i: (layer, 0, 0)),
                  pl.BlockSpec((None, 1, MLA_KV_LORA), lambda i: (layer, 0, 0))],
        out_specs=[pl.BlockSpec((tm, MLA_Q_LORA), lambda i: (i, 0)),
                   pl.BlockSpec((tm, MLA_KV_LORA), lambda i: (i, 0)),
                   pl.BlockSpec((tm, LANES), lambda i: (i, 0)),
                   pl.BlockSpec((tm, 3 * LANES), lambda i: (i, 0))],
        out_shape=[jax.ShapeDtypeStruct((t, MLA_Q_LORA), BF16),
                   jax.ShapeDtypeStruct((t, MLA_KV_LORA), BF16),
                   jax.ShapeDtypeStruct((t, LANES), BF16),
                   jax.ShapeDtypeStruct((t, 3 * LANES), F32)],
        compiler_params=_cparams(("parallel",)),
        name="mla_prep",
    )(proj32, proj32, proj32, ang, gq, gkv)


def _mla_proj_kernel(cqn_ref, ckvn_ref, krot_ref, tab_ref, wq_ref, wkv_ref, q_ref, k_ref, v_ref):
    q = jnp.dot(cqn_ref[...], wq_ref[...], preferred_element_type=F32)
    q_ref[:, :LANES] = q[:, :LANES].astype(q_ref.dtype)
    q_ref[:, LANES:] = _rope128(q[:, LANES:], tab_ref[...]).astype(q_ref.dtype)
    kv = jnp.dot(ckvn_ref[...], wkv_ref[...], preferred_element_type=F32)
    k_ref[:, :LANES] = kv[:, :LANES].astype(k_ref.dtype)
    k_ref[:, LANES:] = krot_ref[...]
    v_ref[...] = kv[:, LANES:].astype(v_ref.dtype)


def _mla_proj(cqn, ckvn, krot, tab, wq, wkv, layer, *, tm=512):
    t = cqn.shape[0]
    tm = min(tm, t)
    tok = lambda w: pl.BlockSpec((tm, w), lambda i, h: (i, 0))
    return pl.pallas_call(
        _mla_proj_kernel,
        grid=(t // tm, MLA_HEADS),
        in_specs=[tok(MLA_Q_LORA), tok(MLA_KV_LORA), tok(LANES), tok(3 * LANES),
                  pl.BlockSpec((None, None, MLA_Q_LORA, MLA_QK), lambda i, h: (layer, h, 0, 0)),
                  pl.BlockSpec((None, None, MLA_KV_LORA, 2 * LANES), lambda i, h: (layer, h, 0, 0))],
        out_specs=[pl.BlockSpec((None, tm, MLA_QK), lambda i, h: (h, i, 0)),
                   pl.BlockSpec((None, tm, MLA_QK), lambda i, h: (h, i, 0)),
                   pl.BlockSpec((None, tm, LANES), lambda i, h: (h, i, 0))],
        out_shape=[jax.ShapeDtypeStruct((MLA_HEADS, t, MLA_QK), BF16),
                   jax.ShapeDtypeStruct((MLA_HEADS, t, MLA_QK), BF16),
                   jax.ShapeDtypeStruct((MLA_HEADS, t, LANES), BF16)],
        compiler_params=_cparams(("parallel", "parallel")),
        name="mla_proj",
    )(cqn, ckvn, krot, tab, wq, wkv)


def _mem_attn_kernel(q_ref, k_ref, v_ref, o_ref):
    scale = MEM_DIM ** -0.5
    outs = []
    for h in range(MEM_HEADS):
        sl = slice(h * MEM_DIM, (h + 1) * MEM_DIM)
        s = lax.dot_general(q_ref[:, sl], k_ref[:, sl], (((1,), (1,)), ((), ())),
                            preferred_element_type=F32) * scale
        p = jnp.exp(s - jnp.max(s, axis=-1, keepdims=True))
        p = p / jnp.sum(p, axis=-1, keepdims=True)
        outs.append(jnp.dot(p.astype(BF16), v_ref[:, sl], preferred_element_type=F32))
    o_ref[...] = jnp.concatenate(outs, axis=1).astype(o_ref.dtype)


def _mem_attn(q, km, vm, *, batch, seq, mem_tokens, tm=512):
    tm = min(tm, seq)
    nt = seq // tm
    w = MEM_HEADS * MEM_DIM
    kv = pl.BlockSpec((mem_tokens, w), lambda i: (i // nt, 0))
    return pl.pallas_call(
        _mem_attn_kernel,
        grid=(batch * nt,),
        in_specs=[pl.BlockSpec((tm, w), lambda i: (i, 0)), kv, kv],
        out_specs=pl.BlockSpec((tm, w), lambda i: (i, 0)),
        out_shape=jax.ShapeDtypeStruct((batch * seq, w), BF16),
        compiler_params=_cparams(("parallel",)),
        name="mem_attn",
    )(q, km, vm)


def _split_w_in(w_in):
    sizes = (1024, 1024, 1024, 1024, 1024, 1024, 1024, FOX_HEADS, MLA_Q_LORA, MLA_KV_LORA, MLA_ROPE)
    offs = [0]
    for s in sizes:
        offs.append(offs[-1] + s)
    hq, hf, hi, hg, fq, fk, fv, ff, cq, ckv, kr = (w_in[:, :, offs[n]:offs[n + 1]] for n in range(len(sizes)))
    w16 = jnp.concatenate([hq, hi, hg, fq, fk, fv], axis=-1).astype(BF16)
    pad = jnp.zeros(w_in.shape[:2] + (2 * LANES - MLA_ROPE - FOX_HEADS,), w_in.dtype)
    w32 = jnp.concatenate([ckv, hf, cq, kr, ff, pad], axis=-1).astype(BF16)
    return w16, w32


def _split_mla_weights(w_uq, w_ukv):
    depth = w_uq.shape[0]
    wq = w_uq.reshape(depth, MLA_Q_LORA, MLA_HEADS, MLA_NOPE + MLA_ROPE).transpose(0, 2, 1, 3)
    wq = jnp.pad(wq, ((0, 0), (0, 0), (0, 0), (0, MLA_QK - MLA_NOPE - MLA_ROPE)))
    wkv = w_ukv.reshape(depth, MLA_KV_LORA, MLA_HEADS, MLA_NOPE + MLA_V).transpose(0, 2, 1, 3)
    return wq.astype(BF16), wkv.astype(BF16)


def kernel(x, mem, positions, w_in, hg_lb_logits, fox_f_bias, mla_q_norm_g, mla_kv_norm_g, w_uq, w_ukv,
           mix_out_g, w_o, mem_norm_g, w_mq, w_mk, w_mv, w_mo, w_ff1, w_ff2, pre_mix_g, post_mix_g,
           pre_mem_g, post_mem_g, pre_ffn_g, post_ffn_g):
    batch, seq, d = x.shape
    depth = w_in.shape[0]
    t = batch * seq
    mem_tokens = mem.shape[1]

    w16, w32 = _split_w_in(w_in)
    wq, wkv = _split_mla_weights(w_uq, w_ukv)
    w_o16, w_mq16, w_mk16, w_mv16, w_mo16 = (w.astype(BF16) for w in (w_o, w_mq, w_mk, w_mv, w_mo))
    w_ff1_16, w_ff2_16 = w_ff1.astype(BF16), w_ff2.astype(BF16)
    row = lambda g: g.astype(F32).reshape(depth, 1, g.shape[-1])
    pre_mix, post_mix, pre_mem, post_mem, pre_ffn, post_ffn = map(
        row, (pre_mix_g, post_mix_g, pre_mem_g, post_mem_g, pre_ffn_g, post_ffn_g))
    mix_g, mem_g, gq, gkv = map(row, (mix_out_g, mem_norm_g, mla_q_norm_g, mla_kv_norm_g))
    fox_bias = jnp.pad(fox_f_bias.astype(F32), ((0, 0), (FF_LANE, LANES - FF_LANE - FOX_HEADS)))
    fox_bias = fox_bias.reshape(depth, 1, LANES)
    llb, l1m = _lower_bounds(hg_lb_logits)
    llb, l1m = row(llb), row(l1m)

    half = MLA_ROPE // 2
    inv = ROPE_THETA ** (-jnp.arange(half, dtype=F32) / half)
    ang = positions.reshape(t, 1).astype(F32) * inv[None, :]
    ang = jnp.concatenate([ang, ang, jnp.zeros((t, LANES - MLA_ROPE), F32)], axis=1)

    h = x.reshape(t, d)
    mem2 = mem.reshape(batch * mem_tokens, d)
    hn = _rms_cast(h, pre_mix, 0, name="rms_first")

    for l in range(depth):
        proj16 = _matmul(hn, w16, l, BF16, tm=1024, tn=1024, name="in_proj16")
        proj32 = _matmul(hn, w32, l, F32, tm=1024, tn=P32_COLS // 2, name="in_proj32")

        mix_a = _hgrn(proj16, proj32, llb, l1m, mix_g, l, batch=batch, seq=seq)

        c_col, c_row = _fox_cum(proj32, fox_bias, l, batch=batch, seq=seq)
        mix_b = _flash(
            proj16, proj16, proj16, mix_g, l,
            q_map=lambda hh, r: (r, P16_FQ + hh),
            kv_map=(lambda hh, r: (r, P16_FK + hh), lambda hh, r: (r, P16_FV + hh)),
            g_map=lambda hh: HG_HEADS + hh, heads=FOX_HEADS, dk=FOX_DIM, bias=(c_col, c_row),
            batch=batch, seq=seq, scale=FOX_DIM ** -0.5, name="fox_attn")

        cqn, ckvn, krot, tab = _mla_prep(proj32, ang, gq, gkv, l)
        q_c, k_c, v_c = _mla_proj(cqn, ckvn, krot, tab, wq, wkv, l)
        mix_c = _flash(
            q_c, k_c, v_c, mix_g, l,
            q_map=lambda hh, r: (hh, r, 0),
            kv_map=(lambda hh, r: (hh, r, 0), lambda hh, r: (hh, r, 0)),
            g_map=lambda hh: HG_HEADS + FOX_HEADS + hh, heads=MLA_HEADS, dk=MLA_QK,
            batch=batch, seq=seq, scale=(MLA_NOPE + MLA_ROPE) ** -0.5, name="mla_attn")

        mix = jnp.concatenate([mix_a, mix_b, mix_c], axis=1)
        y = _matmul(mix, w_o16, l, F32, tm=1024, tn=1024, name="mix_out")
        h, hn = _resid_norm(h, y, post_mix, l, pre_mem, l)

        mem_n = _rms_cast(mem2, mem_g, l, name="rms_mem")
        km = _matmul(mem_n, w_mk16, l, BF16, tm=512, tn=512, name="mem_k")
        vm = _matmul(mem_n, w_mv16, l, BF16, tm=512, tn=512, name="mem_v")
        qm = _matmul(hn, w_mq16, l, BF16, tm=1024, tn=512, name="mem_q")
        om = _mem_attn(qm, km, vm, batch=batch, seq=seq, mem_tokens=mem_tokens)
        y = _matmul(om, w_mo16, l, F32, tm=1024, tn=1024, name="mem_out")
        h, hn = _resid_norm(h, y, post_mem, l, pre_ffn, l)

        u = _matmul(hn, w_ff1_16, l, BF16, tm=1024, tn=1024, relu2=True, name="ffn_up")
        y = _matmul(u, w_ff2_16, l, F32, tm=1024, tn=1024, tk=4096, name="ffn_down")
        if l + 1 < depth:
            h, hn = _resid_norm(h, y, post_ffn, l, pre_mix, l + 1)
        else:
            h, _ = _resid_norm(h, y, post_ffn, l)

    return h.reshape(batch, seq, d)
```

```python
import functools

import jax
import jax.numpy as jnp
from jax import lax
from jax.experimental import pallas as pl
from jax.experimental.pallas import tpu as pltpu

F32 = jnp.float32
BF16 = jnp.bfloat16

D_MODEL = 4096
DEPTH = 4
HG_HEADS = 8
HG_DK = 128
HG_DV = 128
FOX_HEADS = 8
FOX_DIM = 128
MLA_HEADS = 16
MLA_Q_LORA = 768
MLA_KV_LORA = 512
MLA_NOPE = 128
MLA_ROPE = 64
MLA_V = 128
ROPE_THETA = 10000.0
MEM_HEADS = 4
MEM_DIM = 128
EPS = 1e-6
MASK_VALUE = -1e30
LB_FLOOR = 1e-30
LOG2E = 1.4426950408889634

LANES = 128
SUBLANES = 8
HG_CHUNK = 64
ATT_W = 2 * LANES
FLASH_TQ = 1024
FLASH_SUB = 512

P16_HQ, P16_HI, P16_HG = 0, 8, 16
P16_FQKV = 3
P16_COLS = 6 * 1024
P32_HF = 4
P32_MISC = 18
P32_COLS = 20 * LANES
FF_LANE = 64


def _cparams(sem, vmem_mb=None):
    kw = dict(dimension_semantics=sem)
    if vmem_mb is not None:
        kw["vmem_limit_bytes"] = vmem_mb * 1024 * 1024
    return pltpu.CompilerParams(**kw)


def _mm_kernel(a_ref, b_ref, o_ref, *scratch, nk, relu2):
    def finish(r):
        if relu2:
            r = jnp.square(jnp.maximum(r, 0.0))
        o_ref[...] = r.astype(o_ref.dtype)

    if nk == 1:
        finish(jnp.dot(a_ref[...], b_ref[...], preferred_element_type=F32))
        return
    (acc_ref,) = scratch
    k = pl.program_id(2)

    @pl.when(k == 0)
    def _():
        acc_ref[...] = jnp.zeros_like(acc_ref)

    acc_ref[...] += jnp.dot(a_ref[...], b_ref[...], preferred_element_type=F32)

    @pl.when(k == nk - 1)
    def _():
        finish(acc_ref[...])


def _matmul(a, w, layer, out_dtype, *, tm, tn, tk=None, relu2=False, name="mm"):
    m, kdim = a.shape
    n = w.shape[-1]
    tk = kdim if tk is None else min(tk, kdim)
    tm, tn = min(tm, m), min(tn, n)
    assert m % tm == 0 and n % tn == 0 and kdim % tk == 0
    nk = kdim // tk
    scratch = [] if nk == 1 else [pltpu.VMEM((tm, tn), F32)]
    return pl.pallas_call(
        functools.partial(_mm_kernel, nk=nk, relu2=relu2),
        grid=(m // tm, n // tn, nk),
        in_specs=[
            pl.BlockSpec((tm, tk), lambda i, j, k: (i, k)),
            pl.BlockSpec((None, tk, tn), lambda i, j, k: (layer, k, j)),
        ],
        out_specs=pl.BlockSpec((tm, tn), lambda i, j, k: (i, j)),
        out_shape=jax.ShapeDtypeStruct((m, n), out_dtype),
        scratch_shapes=scratch,
        compiler_params=_cparams(("parallel", "parallel", "arbitrary"), 56),
        name=name,
    )(a, w)


def _rms(x, g):
    return x * lax.rsqrt(jnp.mean(x * x, axis=-1, keepdims=True) + EPS) * g


def _rms_cast_kernel(x_ref, g_ref, o_ref):
    o_ref[...] = _rms(x_ref[...].astype(F32), g_ref[...]).astype(o_ref.dtype)


def _rms_cast(x, g, layer, *, tm=256, name="rms_cast"):
    m, d = x.shape
    tm = min(tm, m)
    return pl.pallas_call(
        _rms_cast_kernel,
        grid=(m // tm,),
        in_specs=[pl.BlockSpec((tm, d), lambda i: (i, 0)),
                  pl.BlockSpec((None, 1, d), lambda i: (layer, 0, 0))],
        out_specs=pl.BlockSpec((tm, d), lambda i: (i, 0)),
        out_shape=jax.ShapeDtypeStruct((m, d), BF16),
        compiler_params=_cparams(("parallel",)),
        name=name,
    )(x, g)


def _resid_norm_kernel(h_ref, y_ref, gp_ref, *rest, with_next):
    h_new = h_ref[...] + _rms(y_ref[...], gp_ref[...])
    if with_next:
        gn_ref, ho_ref, hn_ref = rest
        hn_ref[...] = _rms(h_new, gn_ref[...]).astype(hn_ref.dtype)
    else:
        (ho_ref,) = rest
    ho_ref[...] = h_new


def _resid_norm(h, y, g_post, layer, g_next=None, layer_next=None, *, tm=256):
    m, d = h.shape
    tm = min(tm, m)
    row = pl.BlockSpec((tm, d), lambda i: (i, 0))
    with_next = g_next is not None
    in_specs = [row, row, pl.BlockSpec((None, 1, d), lambda i: (layer, 0, 0))]
    args = [h, y, g_post]
    out_shape = [jax.ShapeDtypeStruct((m, d), F32)]
    out_specs = [row]
    if with_next:
        in_specs.append(pl.BlockSpec((None, 1, d), lambda i: (layer_next, 0, 0)))
        args.append(g_next)
        out_shape.append(jax.ShapeDtypeStruct((m, d), BF16))
        out_specs.append(row)
    res = pl.pallas_call(
        functools.partial(_resid_norm_kernel, with_next=with_next),
        grid=(m // tm,),
        in_specs=in_specs,
        out_specs=out_specs,
        out_shape=out_shape,
        compiler_params=_cparams(("parallel",)),
        name="resid_norm",
    )(*args)
    return (res[0], res[1]) if with_next else (res[0], None)


def _lower_bound_kernel(x_ref, llb_ref, l1m_ref):
    depth = x_ref.shape[0]
    rows = [x_ref[i:i + 1, :].astype(F32) for i in range(depth)]
    mx = functools.reduce(jnp.maximum, rows)
    ex = [jnp.exp(r - mx) for r in rows]
    tot = functools.reduce(lambda a, b: a + b, ex)
    p = [e / tot for e in ex]
    cum = p[0]
    for i in range(depth):
        if i > 0:
            cum = cum + p[i]
        lb = jnp.clip(cum - p[0], 0.0, 1.0 - 1e-6)
        llb_ref[i:i + 1, :] = jnp.log(jnp.maximum(lb, LB_FLOOR))
        l1m_ref[i:i + 1, :] = jnp.log1p(-lb)


def _lower_bounds(logits):
    shp = jax.ShapeDtypeStruct(logits.shape, F32)
    return pl.pallas_call(_lower_bound_kernel, out_shape=[shp, shp], name="hgrn_lower_bounds")(logits)


def _hgrn_kernel(q_ref, v_ref, gate_ref, z_ref, llb_ref, l1m_ref, g_ref, o_ref,
                 state_ref, b_scr, k_scr, v_scr, *, n_chunks):
    c_len = HG_CHUNK
    n_rows = c_len // SUBLANES

    @pl.when(pl.program_id(2) == 0)
    def _():
        state_ref[...] = jnp.zeros_like(state_ref)

    llb = llb_ref[...]
    l1m = l1m_ref[...]
    gain = g_ref[...]
    r_i = lax.broadcasted_iota(jnp.int32, (c_len, c_len), 0)
    c_i = lax.broadcasted_iota(jnp.int32, (c_len, c_len), 1)
    tri = (c_i <= r_i).astype(F32)
    sub = lax.broadcasted_iota(jnp.int32, (SUBLANES, LANES), 0)

    def chunk(c, slot, state_t):
        rows = pl.ds(pl.multiple_of(c * c_len, c_len), c_len)
        z = z_ref[rows, :]
        q = q_ref[rows, :].astype(F32)
        v16 = v_ref[rows, :]
        sp = jnp.log1p(jnp.exp(-jnp.abs(z)))
        ls = jnp.minimum(z, 0.0) - sp
        a2 = l1m + ls
        log_f = jnp.maximum(llb, a2) + jnp.log1p(jnp.exp(-jnp.abs(llb - a2)))
        kk = jnp.exp(l1m + (ls - z))
        b = jnp.dot(tri, log_f, preferred_element_type=F32, precision=lax.Precision.HIGHEST) * LOG2E
        b_scr[slot] = b
        k_scr[slot] = kk
        v_scr[slot] = v16.astype(F32)

        q_rows = [q[r * SUBLANES:(r + 1) * SUBLANES, :] for r in range(n_rows)]
        b_rows = [b[r * SUBLANES:(r + 1) * SUBLANES, :] for r in range(n_rows)]
        o_rows = [jnp.zeros((SUBLANES, LANES), F32) for _ in range(n_rows)]
        for g in range(n_rows):
            for i in range(SUBLANES):
                s = g * SUBLANES + i
                bs = b_scr[slot, s:s + 1, :]
                ks = k_scr[slot, s:s + 1, :]
                vs = v_scr[slot, s:s + 1, :]
                for r in range(g, n_rows):
                    e = jnp.exp2(b_rows[r] - bs)
                    if r == g and i > 0:
                        e = jnp.where(sub >= i, e, 0.0)
                    p = e * (q_rows[r] * ks)
                    o_rows[r] = o_rows[r] + jnp.sum(p, axis=-1, keepdims=True) * vs
        o = jnp.concatenate(o_rows, axis=0)

        qe = (q * jnp.exp2(b)).astype(BF16)
        o = o + lax.dot_general(qe, state_t.astype(BF16), (((1,), (1,)), ((), ())),
                                preferred_element_type=F32)
        b_end = b[c_len - 1:c_len, :]
        kd = (kk * jnp.exp2(b_end - b)).astype(BF16)
        upd = lax.dot_general(v16, kd, (((0,), (0,)), ((), ())), preferred_element_type=F32)
        state_t = state_t * jnp.exp2(b_end) + upd

        gate = gate_ref[rows, :].astype(F32)
        y = _rms(o, gain) * (gate * jax.nn.sigmoid(gate))
        o_ref[rows, :] = y.astype(o_ref.dtype)
        return state_t

    def pair(cc, state_t):
        state_t = chunk(2 * cc, 0, state_t)
        return chunk(2 * cc + 1, 1, state_t)

    state_ref[...] = lax.fori_loop(0, n_chunks // 2, pair, state_ref[...])


def _hgrn(proj16, proj32, llb, l1m, mix_g, layer, *, batch, seq, blk=512):
    blk = min(blk, seq)
    ns = seq // blk
    assert (blk // HG_CHUNK) % 2 == 0
    tok = lambda col0: pl.BlockSpec((blk, LANES), lambda b, h, s: (b * ns + s, col0 + h))
    par = pl.BlockSpec((None, 1, LANES), lambda b, h, s: (layer, 0, h))
    return pl.pallas_call(
        functools.partial(_hgrn_kernel, n_chunks=blk // HG_CHUNK),
        grid=(batch, HG_HEADS, ns),
        in_specs=[tok(P16_HQ), tok(P16_HI), tok(P16_HG), tok(P32_HF), par, par, par],
        out_specs=pl.BlockSpec((blk, LANES), lambda b, h, s: (b * ns + s, h)),
        out_shape=jax.ShapeDtypeStruct((batch * seq, HG_HEADS * HG_DV), BF16),
        scratch_shapes=[pltpu.VMEM((HG_DV, HG_DK), F32)] + [pltpu.VMEM((2, HG_CHUNK, LANES), F32)] * 3,
        compiler_params=_cparams(("parallel", "parallel", "arbitrary")),
        name="hgrn2_scan",
    )(proj16, proj16, proj16, proj32, llb, l1m, mix_g)


def _fox_prep_kernel(x_ref, bias_ref, fq_ref, fk_ref, fv_ref, qo_ref, ko_ref, vo_ref, carry_ref, *, blk):
    @pl.when(pl.program_id(1) == 0)
    def _():
        carry_ref[...] = jnp.zeros_like(carry_ref)

    x = x_ref[...] + bias_ref[...]
    lf = jnp.minimum(x, 0.0) - jnp.log1p(jnp.exp(-jnp.abs(x)))
    r_i = lax.broadcasted_iota(jnp.int32, (blk, blk), 0)
    c_i = lax.broadcasted_iota(jnp.int32, (blk, blk), 1)
    tri = (c_i <= r_i).astype(F32)
    cum = jnp.dot(tri, lf, preferred_element_type=F32, precision=lax.Precision.HIGHEST) + carry_ref[...]
    carry_ref[...] = cum[blk - 1:blk, :]
    c2 = cum * LOG2E
    lane = lax.broadcasted_iota(jnp.int32, (blk, LANES), 1)
    ones = jnp.ones((blk, LANES), BF16)
    for h in range(FOX_HEADS):
        col = jnp.broadcast_to(c2[:, FF_LANE + h:FF_LANE + h + 1], (blk, LANES))
        hi = col.astype(BF16).astype(F32)
        r1 = col - hi
        mid = r1.astype(BF16).astype(F32)
        lo = r1 - mid
        pieces = jnp.where((lane == 0) | (lane == 3), hi, jnp.where((lane == 1) | (lane == 4), mid, lo))
        q_ext = jnp.where(lane < 3, pieces, jnp.where(lane < 6, 1.0, 0.0))
        k_ext = jnp.where(lane < 3, 1.0, jnp.where(lane < 6, -pieces, 0.0))
        sl = slice(h * LANES, (h + 1) * LANES)
        qo_ref[h, :, :LANES] = fq_ref[:, sl]
        qo_ref[h, :, LANES:] = q_ext.astype(BF16)
        ko_ref[h, :, :LANES] = fk_ref[:, sl]
        ko_ref[h, :, LANES:] = k_ext.astype(BF16)
        vo_ref[h, :, :LANES] = fv_ref[:, sl]
        vo_ref[h, :, LANES:] = ones


def _fox_prep(proj16, proj32, bias_row, layer, *, batch, seq, blk=256):
    blk = min(blk, seq)
    nb = seq // blk
    t = batch * seq
    w = FOX_HEADS * FOX_DIM
    tok = lambda c: pl.BlockSpec((blk, w), lambda b, j: (b * nb + j, P16_FQKV + c))
    out = pl.BlockSpec((FOX_HEADS, blk, ATT_W), lambda b, j: (0, b * nb + j, 0))
    shp = jax.ShapeDtypeStruct((FOX_HEADS, t, ATT_W), BF16)
    return pl.pallas_call(
        functools.partial(_fox_prep_kernel, blk=blk),
        grid=(batch, nb),
        in_specs=[pl.BlockSpec((blk, LANES), lambda b, j: (b * nb + j, P32_MISC)),
                  pl.BlockSpec((None, 1, LANES), lambda b, j: (layer, 0, 0)),
                  tok(0), tok(1), tok(2)],
        out_specs=[out, out, out],
        out_shape=[shp, shp, shp],
        scratch_shapes=[pltpu.VMEM((1, LANES), F32)],
        compiler_params=_cparams(("parallel", "arbitrary")),
        name="fox_prep",
    )(proj32, bias_row, proj16, proj16, proj16)


def _flash_kernel(q_ref, k_ref, v_ref, g_ref, o_ref, m_scr, acc_scr, *, tq, sub):
    i = pl.program_id(2)
    n_sub = tq // sub
    m_scr[...] = jnp.full_like(m_scr, MASK_VALUE)
    acc_scr[...] = jnp.zeros_like(acc_scr)
    r_i = lax.broadcasted_iota(jnp.int32, (sub, sub), 0)
    c_i = lax.broadcasted_iota(jnp.int32, (sub, sub), 1)

    def step(r, kv0, diagonal):
        rows = slice(r * sub, (r + 1) * sub)
        keys = pl.ds(kv0, sub)
        s = lax.dot_general(q_ref[rows, :], k_ref[keys, :], (((1,), (1,)), ((), ())),
                            preferred_element_type=F32)
        if diagonal:
            s = jnp.where(c_i <= r_i, s, MASK_VALUE)
        m_prev = m_scr[rows, :]
        m_new = jnp.maximum(m_prev, jnp.max(s, axis=-1, keepdims=True))
        alpha = jnp.exp2(m_prev - m_new)
        p = jnp.exp2(s - jnp.concatenate([m_new] * (sub // LANES), axis=1)).astype(BF16)
        pv = jnp.dot(p, v_ref[keys, :], preferred_element_type=F32)
        acc_scr[rows, :] = jnp.concatenate([alpha, alpha], axis=1) * acc_scr[rows, :] + pv
        m_scr[rows, :] = m_new

    def full_chunks(j, carry):
        kv0 = pl.multiple_of(j * sub, sub)
        for r in range(n_sub):
            step(r, kv0, False)
        return carry

    lax.fori_loop(0, i * n_sub, full_chunks, 0)
    base = pl.multiple_of(i * tq, tq)
    for r in range(n_sub):
        for jj in range(r + 1):
            step(r, base + jj * sub, jj == r)

    acc = acc_scr[...]
    o = acc[:, :LANES] / acc[:, LANES:]
    o_ref[...] = _rms(o, g_ref[...]).astype(o_ref.dtype)


def _flash(q, k, v, gain, layer, g_off, *, batch, seq, name):
    heads, t, _ = q.shape
    tq, sub = min(FLASH_TQ, seq), min(FLASH_SUB, seq)
    nq = seq // tq
    kv = pl.BlockSpec((None, seq, ATT_W), lambda b, h, i: (h, b, 0))
    return pl.pallas_call(
        functools.partial(_flash_kernel, tq=tq, sub=sub),
        grid=(batch, heads, nq),
        in_specs=[pl.BlockSpec((None, tq, ATT_W), lambda b, h, i: (h, b * nq + i, 0)), kv, kv,
                  pl.BlockSpec((None, 1, LANES), lambda b, h, i: (layer, 0, g_off + h))],
        out_specs=pl.BlockSpec((tq, LANES), lambda b, h, i: (b * nq + i, h)),
        out_shape=jax.ShapeDtypeStruct((t, heads * LANES), BF16),
        scratch_shapes=[pltpu.VMEM((tq, LANES), F32), pltpu.VMEM((tq, ATT_W), F32)],
        compiler_params=_cparams(("parallel", "parallel", "arbitrary")),
        name=name,
    )(q, k, v, gain)


def _rope128(x, tab):
    c, s1, s2 = tab[:, :LANES], tab[:, LANES:2 * LANES], tab[:, 2 * LANES:]
    half = MLA_ROPE // 2
    return x * c + pltpu.roll(x, LANES - half, axis=1) * s1 + pltpu.roll(x, half, axis=1) * s2


def _mla_prep_kernel(ckv_ref, cq_ref, misc_ref, ang_ref, gq_ref, gkv_ref,
                     cqn_ref, ckvn_ref, krot_ref, tab_ref):
    cqn_ref[...] = _rms(cq_ref[...], gq_ref[...]).astype(cqn_ref.dtype)
    ckvn_ref[...] = _rms(ckv_ref[...], gkv_ref[...]).astype(ckvn_ref.dtype)
    ang = ang_ref[...]
    cos, sin = jnp.cos(ang), jnp.sin(ang)
    lane = lax.broadcasted_iota(jnp.int32, ang.shape, 1)
    half = MLA_ROPE // 2
    tab = jnp.concatenate([
        jnp.where(lane < MLA_ROPE, cos, 0.0),
        jnp.where(lane < half, -sin, 0.0),
        jnp.where((lane >= half) & (lane < MLA_ROPE), sin, 0.0)], axis=1)
    tab_ref[...] = tab
    krot_ref[...] = _rope128(misc_ref[...], tab).astype(krot_ref.dtype)


def _mla_prep(proj32, ang, gq, gkv, layer, *, tm=512):
    t = proj32.shape[0]
    tm = min(tm, t)
    return pl.pallas_call(
        _mla_prep_kernel,
        grid=(t // tm,),
        in_specs=[pl.BlockSpec((tm, MLA_KV_LORA), lambda i: (i, 0)),
                  pl.BlockSpec((tm, MLA_Q_LORA), lambda i: (i, 2)),
                  pl.BlockSpec((tm, LANES), lambda i: (i, P32_MISC)),
                  pl.BlockSpec((tm, LANES), lambda i: (i, 0)),
                  pl.BlockSpec((None, 1, MLA_Q_LORA), lambda i: (layer, 0, 0)),
                  pl.BlockSpec((None, 1, MLA_KV_LORA), lambda i: (layer, 0, 0))],
        out_specs=[pl.BlockSpec((tm, MLA_Q_LORA), lambda i: (i, 0)),
                   pl.BlockSpec((tm, MLA_KV_LORA), lambda i: (i, 0)),
                   pl.BlockSpec((tm, LANES), lambda i: (i, 0)),
                   pl.BlockSpec((tm, 3 * LANES), lambda i: (i, 0))],
        out_shape=[jax.ShapeDtypeStruct((t, MLA_Q_LORA), BF16),
                   jax.ShapeDtypeStruct((t, MLA_KV_LORA), BF16),
                   jax.ShapeDtypeStruct((t, LANES), BF16),
                   jax.ShapeDtypeStruct((t, 3 * LANES), F32)],
        compiler_params=_cparams(("parallel",)),
        name="mla_prep",
    )(proj32, proj32, proj32, ang, gq, gkv)


def _mla_proj_kernel(cqn_ref, ckvn_ref, krot_ref, tab_ref, wq_ref, wkv_ref, q_ref, k_ref, v_ref):
    q = jnp.dot(cqn_ref[...], wq_ref[...], preferred_element_type=F32)
    q_ref[:, :LANES] = q[:, :LANES].astype(q_ref.dtype)
    q_ref[:, LANES:] = _rope128(q[:, LANES:], tab_ref[...]).astype(q_ref.dtype)
    kv = jnp.dot(ckvn_ref[...], wkv_ref[...], preferred_element_type=F32)
    k_ref[:, :LANES] = kv[:, :LANES].astype(k_ref.dtype)
    k_ref[:, LANES:] = krot_ref[...]
    v_ref[:, :LANES] = kv[:, LANES:].astype(v_ref.dtype)
    v_ref[:, LANES:] = jnp.ones((v_ref.shape[0], LANES), v_ref.dtype)


def _mla_proj(cqn, ckvn, krot, tab, wq, wkv, layer, *, tm=2048):
    t = cqn.shape[0]
    tm = min(tm, t)
    tok = lambda w: pl.BlockSpec((tm, w), lambda i, h: (i, 0))
    out = pl.BlockSpec((None, tm, ATT_W), lambda i, h: (h, i, 0))
    shp = jax.ShapeDtypeStruct((MLA_HEADS, t, ATT_W), BF16)
    return pl.pallas_call(
        _mla_proj_kernel,
        grid=(t // tm, MLA_HEADS),
        in_specs=[tok(MLA_Q_LORA), tok(MLA_KV_LORA), tok(LANES), tok(3 * LANES),
                  pl.BlockSpec((None, None, MLA_Q_LORA, ATT_W), lambda i, h: (layer, h, 0, 0)),
                  pl.BlockSpec((None, None, MLA_KV_LORA, 2 * LANES), lambda i, h: (layer, h, 0, 0))],
        out_specs=[out, out, out],
        out_shape=[shp, shp, shp],
        compiler_params=_cparams(("parallel", "parallel")),
        name="mla_proj",
    )(cqn, ckvn, krot, tab, wq, wkv)


def _mem_attn_kernel(q_ref, k_ref, v_ref, o_ref):
    scale = MEM_DIM ** -0.5
    outs = []
    for h in range(MEM_HEADS):
        sl = slice(h * MEM_DIM, (h + 1) * MEM_DIM)
        s = lax.dot_general(q_ref[:, sl], k_ref[:, sl], (((1,), (1,)), ((), ())),
                            preferred_element_type=F32) * scale
        p = jnp.exp(s - jnp.max(s, axis=-1, keepdims=True))
        p = p / jnp.sum(p, axis=-1, keepdims=True)
        outs.append(jnp.dot(p.astype(BF16), v_ref[:, sl], preferred_element_type=F32))
    o_ref[...] = jnp.concatenate(outs, axis=1).astype(o_ref.dtype)


def _mem_attn(q, km, vm, *, batch, seq, mem_tokens, tm=512):
    tm = min(tm, seq)
    nt = seq // tm
    w = MEM_HEADS * MEM_DIM
    kv = pl.BlockSpec((mem_tokens, w), lambda i: (i // nt, 0))
    return pl.pallas_call(
        _mem_attn_kernel,
        grid=(batch * nt,),
        in_specs=[pl.BlockSpec((tm, w), lambda i: (i, 0)), kv, kv],
        out_specs=pl.BlockSpec((tm, w), lambda i: (i, 0)),
        out_shape=jax.ShapeDtypeStruct((batch * seq, w), BF16),
        compiler_params=_cparams(("parallel",)),
        name="mem_attn",
    )(q, km, vm)


def _split_w_in(w_in):
    sizes = (1024, 1024, 1024, 1024, 1024, 1024, 1024, FOX_HEADS, MLA_Q_LORA, MLA_KV_LORA, MLA_ROPE)
    offs = [0]
    for s in sizes:
        offs.append(offs[-1] + s)
    hq, hf, hi, hg, fq, fk, fv, ff, cq, ckv, kr = (w_in[:, :, offs[n]:offs[n + 1]] for n in range(len(sizes)))
    fq = fq * (FOX_DIM ** -0.5 * LOG2E)
    w16 = jnp.concatenate([hq, hi, hg, fq, fk, fv], axis=-1).astype(BF16)
    pad = jnp.zeros(w_in.shape[:2] + (2 * LANES - MLA_ROPE - FOX_HEADS,), w_in.dtype)
    w32 = jnp.concatenate([ckv, hf, cq, kr, ff, pad], axis=-1).astype(BF16)
    return w16, w32


def _split_mla_weights(w_uq, w_ukv):
    depth = w_uq.shape[0]
    wq = w_uq.reshape(depth, MLA_Q_LORA, MLA_HEADS, MLA_NOPE + MLA_ROPE).transpose(0, 2, 1, 3)
    wq = wq * ((MLA_NOPE + MLA_ROPE) ** -0.5 * LOG2E)
    wq = jnp.pad(wq, ((0, 0), (0, 0), (0, 0), (0, ATT_W - MLA_NOPE - MLA_ROPE)))
    wkv = w_ukv.reshape(depth, MLA_KV_LORA, MLA_HEADS, MLA_NOPE + MLA_V).transpose(0, 2, 1, 3)
    return wq.astype(BF16), wkv.astype(BF16)


def kernel(x, mem, positions, w_in, hg_lb_logits, fox_f_bias, mla_q_norm_g, mla_kv_norm_g, w_uq, w_ukv,
           mix_out_g, w_o, mem_norm_g, w_mq, w_mk, w_mv, w_mo, w_ff1, w_ff2, pre_mix_g, post_mix_g,
           pre_mem_g, post_mem_g, pre_ffn_g, post_ffn_g):
    batch, seq, d = x.shape
    depth = w_in.shape[0]
    t = batch * seq
    mem_tokens = mem.shape[1]

    w16, w32 = _split_w_in(w_in)
    wq, wkv = _split_mla_weights(w_uq, w_ukv)
    w_o16, w_mq16, w_mk16, w_mv16, w_mo16 = (w.astype(BF16) for w in (w_o, w_mq, w_mk, w_mv, w_mo))
    w_ff1_16, w_ff2_16 = w_ff1.astype(BF16), w_ff2.astype(BF16)
    row = lambda g: g.astype(F32).reshape(depth, 1, g.shape[-1])
    pre_mix, post_mix, pre_mem, post_mem, pre_ffn, post_ffn = map(
        row, (pre_mix_g, post_mix_g, pre_mem_g, post_mem_g, pre_ffn_g, post_ffn_g))
    mix_g, mem_g, gq, gkv = map(row, (mix_out_g, mem_norm_g, mla_q_norm_g, mla_kv_norm_g))
    fox_bias = jnp.pad(fox_f_bias.astype(F32), ((0, 0), (FF_LANE, LANES - FF_LANE - FOX_HEADS)))
    fox_bias = fox_bias.reshape(depth, 1, LANES)
    llb, l1m = _lower_bounds(hg_lb_logits)
    llb, l1m = row(llb), row(l1m)

    half = MLA_ROPE // 2
    inv = ROPE_THETA ** (-jnp.arange(half, dtype=F32) / half)
    ang = positions.reshape(t, 1).astype(F32) * inv[None, :]
    ang = jnp.concatenate([ang, ang, jnp.zeros((t, LANES - MLA_ROPE), F32)], axis=1)

    h = x.reshape(t, d)
    mem2 = mem.reshape(batch * mem_tokens, d)
    hn = _rms_cast(h, pre_mix, 0, name="rms_first")

    for l in range(depth):
        proj16 = _matmul(hn, w16, l, BF16, tm=1024, tn=1024, name="in_proj16")
        proj32 = _matmul(hn, w32, l, F32, tm=1024, tn=P32_COLS // 2, name="in_proj32")

        mix_a = _hgrn(proj16, proj32, llb, l1m, mix_g, l, batch=batch, seq=seq)

        fq, fk, fv = _fox_prep(proj16, proj32, fox_bias, l, batch=batch, seq=seq)
        mix_b = _flash(fq, fk, fv, mix_g, l, HG_HEADS, batch=batch, seq=seq, name="fox_attn")

        cqn, ckvn, krot, tab = _mla_prep(proj32, ang, gq, gkv, l)
        q_c, k_c, v_c = _mla_proj(cqn, ckvn, krot, tab, wq, wkv, l)
        mix_c = _flash(q_c, k_c, v_c, mix_g, l, HG_HEADS + FOX_HEADS, batch=batch, seq=seq, name="mla_attn")

        mix = jnp.concatenate([mix_a, mix_b, mix_c], axis=1)
        y = _matmul(mix, w_o16, l, F32, tm=1024, tn=1024, name="mix_out")
        h, hn = _resid_norm(h, y, post_mix, l, pre_mem, l)

        mem_n = _rms_cast(mem2, mem_g, l, name="rms_mem")
        km = _matmul(mem_n, w_mk16, l, BF16, tm=512, tn=512, name="mem_k")
        vm = _matmul(mem_n, w_mv16, l, BF16, tm=512, tn=512, name="mem_v")
        qm = _matmul(hn, w_mq16, l, BF16, tm=1024, tn=512, name="mem_q")
        om = _mem_attn(qm, km, vm, batch=batch, seq=seq, mem_tokens=mem_tokens)
        y = _matmul(om, w_mo16, l, F32, tm=1024, tn=1024, name="mem_out")
        h, hn = _resid_norm(h, y, post_mem, l, pre_ffn, l)

        u = _matmul(hn, w_ff1_16, l, BF16, tm=1024, tn=1024, relu2=True, name="ffn_up")
        y = _matmul(u, w_ff2_16, l, F32, tm=1024, tn=1024, tk=4096, name="ffn_down")
        if l + 1 < depth:
            h, hn = _resid_norm(h, y, post_ffn, l, pre_mix, l + 1)
        else:
            h, _ = _resid_norm(h, y, post_ffn, l)

    return h.reshape(batch, seq, d)
```

```python
import functools

import jax
import jax.numpy as jnp
from jax import lax
from jax.experimental import pallas as pl
from jax.experimental.pallas import tpu as pltpu

F32 = jnp.float32
BF16 = jnp.bfloat16

D_MODEL = 4096
DEPTH = 4
HG_HEADS = 8
HG_DK = 128
HG_DV = 128
FOX_HEADS = 8
FOX_DIM = 128
MLA_HEADS = 16
MLA_Q_LORA = 768
MLA_KV_LORA = 512
MLA_NOPE = 128
MLA_ROPE = 64
MLA_V = 128
ROPE_THETA = 10000.0
MEM_HEADS = 4
MEM_DIM = 128
EPS = 1e-6
MASK_VALUE = -1e30
LB_FLOOR = 1e-30
LOG2E = 1.4426950408889634

LANES = 128
SUBLANES = 8
HG_CHUNK = 64
HG_SUB = 16
HG_UNROLL = 8
ATT_W = 2 * LANES
FLASH_TQ = 2048
FLASH_SUB = 512

P16_HQ, P16_HI, P16_HG = 0, 8, 16
P16_FQKV = 3
P16_COLS = 6 * 1024
P32_HF = 4
P32_MISC = 18
P32_COLS = 20 * LANES
FF_LANE = 64


def _cparams(sem, vmem_mb=None):
    kw = dict(dimension_semantics=sem)
    if vmem_mb is not None:
        kw["vmem_limit_bytes"] = vmem_mb * 1024 * 1024
    return pltpu.CompilerParams(**kw)


def _mm_kernel(a_ref, b_ref, o_ref, *scratch, nk, relu2):
    def finish(r):
        if relu2:
            r = jnp.square(jnp.maximum(r, 0.0))
        o_ref[...] = r.astype(o_ref.dtype)

    if nk == 1:
        finish(jnp.dot(a_ref[...], b_ref[...], preferred_element_type=F32))
        return
    (acc_ref,) = scratch
    k = pl.program_id(2)

    @pl.when(k == 0)
    def _():
        acc_ref[...] = jnp.zeros_like(acc_ref)

    acc_ref[...] += jnp.dot(a_ref[...], b_ref[...], preferred_element_type=F32)

    @pl.when(k == nk - 1)
    def _():
        finish(acc_ref[...])


def _matmul(a, w, layer, out_dtype, *, tm, tn, tk=None, relu2=False, name="mm"):
    m, kdim = a.shape
    n = w.shape[-1]
    tk = kdim if tk is None else min(tk, kdim)
    tm, tn = min(tm, m), min(tn, n)
    assert m % tm == 0 and n % tn == 0 and kdim % tk == 0
    nk = kdim // tk
    scratch = [] if nk == 1 else [pltpu.VMEM((tm, tn), F32)]
    return pl.pallas_call(
        functools.partial(_mm_kernel, nk=nk, relu2=relu2),
        grid=(m // tm, n // tn, nk),
        in_specs=[
            pl.BlockSpec((tm, tk), lambda i, j, k: (i, k)),
            pl.BlockSpec((None, tk, tn), lambda i, j, k: (layer, k, j)),
        ],
        out_specs=pl.BlockSpec((tm, tn), lambda i, j, k: (i, j)),
        out_shape=jax.ShapeDtypeStruct((m, n), out_dtype),
        scratch_shapes=scratch,
        compiler_params=_cparams(("parallel", "parallel", "arbitrary"), 56),
        name=name,
    )(a, w)


def _mm_parts_kernel(*refs, widths):
    a_refs, b_ref, o_ref = refs[:len(widths)], refs[len(widths)], refs[len(widths) + 1]
    acc, off = None, 0
    for a_ref, wd in zip(a_refs, widths):
        r = jnp.dot(a_ref[...], b_ref[off:off + wd, :], preferred_element_type=F32)
        acc = r if acc is None else acc + r
        off += wd
    o_ref[...] = acc.astype(o_ref.dtype)


def _matmul_parts(parts, w, layer, out_dtype, *, tm, tn, name):
    m = parts[0].shape[0]
    widths = tuple(p.shape[1] for p in parts)
    kdim, n = sum(widths), w.shape[-1]
    tm, tn = min(tm, m), min(tn, n)
    assert m % tm == 0 and n % tn == 0 and w.shape[-2] == kdim
    return pl.pallas_call(
        functools.partial(_mm_parts_kernel, widths=widths),
        grid=(m // tm, n // tn),
        in_specs=[pl.BlockSpec((tm, wd), lambda i, j: (i, 0)) for wd in widths]
        + [pl.BlockSpec((None, kdim, tn), lambda i, j: (layer, 0, j))],
        out_specs=pl.BlockSpec((tm, tn), lambda i, j: (i, j)),
        out_shape=jax.ShapeDtypeStruct((m, n), out_dtype),
        compiler_params=_cparams(("parallel", "parallel"), 56),
        name=name,
    )(*parts, w)


def _rms(x, g):
    return x * lax.rsqrt(jnp.mean(x * x, axis=-1, keepdims=True) + EPS) * g


def _rms_cast_kernel(x_ref, g_ref, o_ref):
    o_ref[...] = _rms(x_ref[...].astype(F32), g_ref[...]).astype(o_ref.dtype)


def _rms_cast(x, g, layer, *, tm=256, name="rms_cast"):
    m, d = x.shape
    tm = min(tm, m)
    return pl.pallas_call(
        _rms_cast_kernel,
        grid=(m // tm,),
        in_specs=[pl.BlockSpec((tm, d), lambda i: (i, 0)),
                  pl.BlockSpec((None, 1, d), lambda i: (layer, 0, 0))],
        out_specs=pl.BlockSpec((tm, d), lambda i: (i, 0)),
        out_shape=jax.ShapeDtypeStruct((m, d), BF16),
        compiler_params=_cparams(("parallel",)),
        name=name,
    )(x, g)


def _resid_norm_kernel(h_ref, y_ref, gp_ref, *rest, with_next):
    h_new = h_ref[...] + _rms(y_ref[...], gp_ref[...])
    if with_next:
        gn_ref, ho_ref, hn_ref = rest
        hn_ref[...] = _rms(h_new, gn_ref[...]).astype(hn_ref.dtype)
    else:
        (ho_ref,) = rest
    ho_ref[...] = h_new


def _resid_norm(h, y, g_post, layer, g_next=None, layer_next=None, *, tm=256):
    m, d = h.shape
    tm = min(tm, m)
    row = pl.BlockSpec((tm, d), lambda i: (i, 0))
    with_next = g_next is not None
    in_specs = [row, row, pl.BlockSpec((None, 1, d), lambda i: (layer, 0, 0))]
    args = [h, y, g_post]
    out_shape = [jax.ShapeDtypeStruct((m, d), F32)]
    out_specs = [row]
    if with_next:
        in_specs.append(pl.BlockSpec((None, 1, d), lambda i: (layer_next, 0, 0)))
        args.append(g_next)
        out_shape.append(jax.ShapeDtypeStruct((m, d), BF16))
        out_specs.append(row)
    res = pl.pallas_call(
        functools.partial(_resid_norm_kernel, with_next=with_next),
        grid=(m // tm,),
        in_specs=in_specs,
        out_specs=out_specs,
        out_shape=out_shape,
        compiler_params=_cparams(("parallel",)),
        name="resid_norm",
    )(*args)
    return (res[0], res[1]) if with_next else (res[0], None)


def _lower_bound_kernel(x_ref, llb_ref, l1m_ref):
    depth = x_ref.shape[0]
    rows = [x_ref[i:i + 1, :].astype(F32) for i in range(depth)]
    mx = functools.reduce(jnp.maximum, rows)
    ex = [jnp.exp(r - mx) for r in rows]
    tot = functools.reduce(lambda a, b: a + b, ex)
    p = [e / tot for e in ex]
    cum = p[0]
    for i in range(depth):
        if i > 0:
            cum = cum + p[i]
        lb = jnp.clip(cum - p[0], 0.0, 1.0 - 1e-6)
        llb_ref[i:i + 1, :] = jnp.log(jnp.maximum(lb, LB_FLOOR))
        l1m_ref[i:i + 1, :] = jnp.log1p(-lb)


def _lower_bounds(logits):
    shp = jax.ShapeDtypeStruct(logits.shape, F32)
    return pl.pallas_call(_lower_bound_kernel, out_shape=[shp, shp], name="hgrn_lower_bounds")(logits)


def _hgrn_kernel(q_ref, v_ref, gate_ref, z_ref, llb_ref, l1m_ref, g_ref, o_ref,
                 state_ref, b_scr, k_scr, v_scr, *, n_chunks):
    c_len = HG_CHUNK
    n_rows = c_len // SUBLANES

    @pl.when(pl.program_id(2) == 0)
    def _():
        state_ref[...] = jnp.zeros_like(state_ref)

    llb = llb_ref[...]
    l1m = l1m_ref[...]
    gain = g_ref[...]
    r_i = lax.broadcasted_iota(jnp.int32, (c_len, c_len), 0)
    c_i = lax.broadcasted_iota(jnp.int32, (c_len, c_len), 1)
    tri = (c_i <= r_i).astype(F32)
    sub = lax.broadcasted_iota(jnp.int32, (SUBLANES, LANES), 0)

    def chunk(c, slot, state_t):
        rows = pl.ds(pl.multiple_of(c * c_len, c_len), c_len)
        z = z_ref[rows, :]
        q = q_ref[rows, :].astype(F32)
        v16 = v_ref[rows, :]
        sp = jnp.log1p(jnp.exp(-jnp.abs(z)))
        ls = jnp.minimum(z, 0.0) - sp
        a2 = l1m + ls
        log_f = jnp.maximum(llb, a2) + jnp.log1p(jnp.exp(-jnp.abs(llb - a2)))
        kk = jnp.exp(l1m + (ls - z))
        b = jnp.dot(tri, log_f, preferred_element_type=F32, precision=lax.Precision.HIGHEST) * LOG2E
        b_scr[slot] = b
        k_scr[slot] = kk
        v_scr[slot] = v16.astype(F32)

        q_rows = [q[r * SUBLANES:(r + 1) * SUBLANES, :] for r in range(n_rows)]
        b_rows = [b[r * SUBLANES:(r + 1) * SUBLANES, :] for r in range(n_rows)]
        o_rows = [jnp.zeros((SUBLANES, LANES), F32) for _ in range(n_rows)]
        rows_per_sb = HG_SUB // SUBLANES
        for g in range(n_rows):
            r_hi = (g // rows_per_sb + 1) * rows_per_sb
            for i in range(SUBLANES):
                s = g * SUBLANES + i
                bs = b_scr[slot, s:s + 1, :]
                ks = k_scr[slot, s:s + 1, :]
                vs = v_scr[slot, s:s + 1, :]
                for r in range(g, r_hi):
                    e = jnp.exp2(b_rows[r] - bs)
                    if r == g and i > 0:
                        e = jnp.where(sub >= i, e, 0.0)
                    p = e * (q_rows[r] * ks)
                    o_rows[r] = o_rows[r] + jnp.sum(p, axis=-1, keepdims=True) * vs
        o = jnp.concatenate(o_rows, axis=0)

        n_sb = c_len // HG_SUB
        bend = [b[(jb + 1) * HG_SUB - 1:(jb + 1) * HG_SUB, :] for jb in range(n_sb)]
        bend_rows = jnp.concatenate([jnp.broadcast_to(e_, (HG_SUB, LANES)) for e_ in bend], axis=0)
        khat = (kk * jnp.exp2(bend_rows - b)).astype(BF16)
        a_stack = jnp.concatenate(
            [q * jnp.exp2(jnp.minimum(b - bend[jb], 0.0)) for jb in range(n_sb - 1)], axis=0).astype(BF16)
        r_all = lax.dot_general(a_stack, khat, (((1,), (1,)), ((), ())), preferred_element_type=F32)
        s_off = jnp.zeros((c_len, c_len), F32)
        for jb in range(n_sb - 1):
            take = (c_i >= jb * HG_SUB) & (c_i < (jb + 1) * HG_SUB) & (r_i >= (jb + 1) * HG_SUB)
            s_off = jnp.where(take, r_all[jb * c_len:(jb + 1) * c_len, :], s_off)
        o = o + jnp.dot(s_off.astype(BF16), v16, preferred_element_type=F32)

        qe = (q * jnp.exp2(b)).astype(BF16)
        o = o + lax.dot_general(qe, state_t.astype(BF16), (((1,), (1,)), ((), ())),
                                preferred_element_type=F32)
        b_end = b[c_len - 1:c_len, :]
        kd = (kk * jnp.exp2(b_end - b)).astype(BF16)
        upd = lax.dot_general(v16, kd, (((0,), (0,)), ((), ())), preferred_element_type=F32)
        state_t = state_t * jnp.exp2(b_end) + upd

        gate = gate_ref[rows, :].astype(F32)
        y = _rms(o, gain) * (gate * jax.nn.sigmoid(gate))
        o_ref[rows, :] = y.astype(o_ref.dtype)
        return state_t

    def trip(cc, state_t):
        for slot in range(HG_UNROLL):
            state_t = chunk(HG_UNROLL * cc + slot, slot, state_t)
        return state_t

    state_ref[...] = lax.fori_loop(0, n_chunks // HG_UNROLL, trip, state_ref[...])


def _hgrn(proj16, proj32, llb, l1m, mix_g, layer, *, batch, seq, blk=512):
    blk = min(blk, seq)
    ns = seq // blk
    assert (blk // HG_CHUNK) % HG_UNROLL == 0
    tok = lambda col0: pl.BlockSpec((blk, LANES), lambda b, h, s: (b * ns + s, col0 + h))
    par = pl.BlockSpec((None, 1, LANES), lambda b, h, s: (layer, 0, h))
    return pl.pallas_call(
        functools.partial(_hgrn_kernel, n_chunks=blk // HG_CHUNK),
        grid=(batch, HG_HEADS, ns),
        in_specs=[tok(P16_HQ), tok(P16_HI), tok(P16_HG), tok(P32_HF), par, par, par],
        out_specs=pl.BlockSpec((blk, LANES), lambda b, h, s: (b * ns + s, h)),
        out_shape=jax.ShapeDtypeStruct((batch * seq, HG_HEADS * HG_DV), BF16),
        scratch_shapes=[pltpu.VMEM((HG_DV, HG_DK), F32)] + [pltpu.VMEM((HG_UNROLL, HG_CHUNK, LANES), F32)] * 3,
        compiler_params=_cparams(("parallel", "parallel", "arbitrary")),
        name="hgrn2_scan",
    )(proj16, proj16, proj16, proj32, llb, l1m, mix_g)


def _fox_prep_kernel(x_ref, bias_ref, fq_ref, fk_ref, fv_ref, qo_ref, ko_ref, vo_ref, carry_ref, *, blk):
    @pl.when(pl.program_id(1) == 0)
    def _():
        carry_ref[...] = jnp.zeros_like(carry_ref)

    x = x_ref[...] + bias_ref[...]
    lf = jnp.minimum(x, 0.0) - jnp.log1p(jnp.exp(-jnp.abs(x)))
    r_i = lax.broadcasted_iota(jnp.int32, (blk, blk), 0)
    c_i = lax.broadcasted_iota(jnp.int32, (blk, blk), 1)
    tri = (c_i <= r_i).astype(F32)
    cum = jnp.dot(tri, lf, preferred_element_type=F32, precision=lax.Precision.HIGHEST) + carry_ref[...]
    carry_ref[...] = cum[blk - 1:blk, :]
    c2 = cum * LOG2E
    lane = lax.broadcasted_iota(jnp.int32, (blk, LANES), 1)
    ones = jnp.ones((blk, LANES), BF16)
    for h in range(FOX_HEADS):
        col = jnp.broadcast_to(c2[:, FF_LANE + h:FF_LANE + h + 1], (blk, LANES))
        hi = col.astype(BF16).astype(F32)
        r1 = col - hi
        mid = r1.astype(BF16).astype(F32)
        lo = r1 - mid
        pieces = jnp.where((lane == 0) | (lane == 3), hi, jnp.where((lane == 1) | (lane == 4), mid, lo))
        q_ext = jnp.where(lane < 3, pieces, jnp.where(lane < 6, 1.0, 0.0))
        k_ext = jnp.where(lane < 3, 1.0, jnp.where(lane < 6, -pieces, 0.0))
        sl = slice(h * LANES, (h + 1) * LANES)
        qo_ref[h, :, :LANES] = fq_ref[:, sl]
        qo_ref[h, :, LANES:] = q_ext.astype(BF16)
        ko_ref[h, :, :LANES] = fk_ref[:, sl]
        ko_ref[h, :, LANES:] = k_ext.astype(BF16)
        vo_ref[h, :, :LANES] = fv_ref[:, sl]
        vo_ref[h, :, LANES:] = ones


def _fox_prep(proj16, proj32, bias_row, layer, *, batch, seq, blk=256):
    blk = min(blk, seq)
    nb = seq // blk
    t = batch * seq
    w = FOX_HEADS * FOX_DIM
    tok = lambda c: pl.BlockSpec((blk, w), lambda b, j: (b * nb + j, P16_FQKV + c))
    out = pl.BlockSpec((FOX_HEADS, blk, ATT_W), lambda b, j: (0, b * nb + j, 0))
    shp = jax.ShapeDtypeStruct((FOX_HEADS, t, ATT_W), BF16)
    return pl.pallas_call(
        functools.partial(_fox_prep_kernel, blk=blk),
        grid=(batch, nb),
        in_specs=[pl.BlockSpec((blk, LANES), lambda b, j: (b * nb + j, P32_MISC)),
                  pl.BlockSpec((None, 1, LANES), lambda b, j: (layer, 0, 0)),
                  tok(0), tok(1), tok(2)],
        out_specs=[out, out, out],
        out_shape=[shp, shp, shp],
        scratch_shapes=[pltpu.VMEM((1, LANES), F32)],
        compiler_params=_cparams(("parallel", "arbitrary")),
        name="fox_prep",
    )(proj32, bias_row, proj16, proj16, proj16)


def _flash_kernel(q_ref, k_ref, v_ref, g_ref, o_ref, m_scr, acc_scr, *, tq, sub):
    i = pl.program_id(2)
    n_sub = tq // sub
    m_scr[...] = jnp.full_like(m_scr, MASK_VALUE)
    acc_scr[...] = jnp.zeros_like(acc_scr)
    r_i = lax.broadcasted_iota(jnp.int32, (sub, sub), 0)
    c_i = lax.broadcasted_iota(jnp.int32, (sub, sub), 1)

    def step(r, kv0, diagonal):
        rows = slice(r * sub, (r + 1) * sub)
        keys = pl.ds(kv0, sub)
        s = lax.dot_general(q_ref[rows, :], k_ref[keys, :], (((1,), (1,)), ((), ())),
                            preferred_element_type=F32)
        if diagonal:
            s = jnp.where(c_i <= r_i, s, MASK_VALUE)
        m_prev = m_scr[rows, :]
        m_new = jnp.maximum(m_prev, jnp.max(s, axis=-1, keepdims=True))
        alpha = jnp.exp2(m_prev - m_new)
        p = jnp.exp2(s - jnp.concatenate([m_new] * (sub // LANES), axis=1)).astype(BF16)
        pv = jnp.dot(p, v_ref[keys, :], preferred_element_type=F32)
        acc_scr[rows, :] = jnp.concatenate([alpha, alpha], axis=1) * acc_scr[rows, :] + pv
        m_scr[rows, :] = m_new

    kv_per_trip = 2 if n_sub % 2 == 0 else 1

    def full_chunks(j, carry):
        kv0 = pl.multiple_of(j * (kv_per_trip * sub), kv_per_trip * sub)
        for jj in range(kv_per_trip):
            for r in range(n_sub):
                step(r, kv0 + jj * sub, False)
        return carry

    lax.fori_loop(0, i * n_sub // kv_per_trip, full_chunks, 0)
    base = pl.multiple_of(i * tq, tq)
    for r in range(n_sub):
        for jj in range(r + 1):
            step(r, base + jj * sub, jj == r)

    acc = acc_scr[...]
    o = acc[:, :LANES] / acc[:, LANES:]
    o_ref[...] = _rms(o, g_ref[...]).astype(o_ref.dtype)


def _flash(q, k, v, gain, layer, g_off, *, batch, seq, name):
    heads, t, _ = q.shape
    tq, sub = min(FLASH_TQ, seq), min(FLASH_SUB, seq)
    nq = seq // tq
    kv = pl.BlockSpec((None, seq, ATT_W), lambda b, h, i: (h, b, 0))
    return pl.pallas_call(
        functools.partial(_flash_kernel, tq=tq, sub=sub),
        grid=(batch, heads, nq),
        in_specs=[pl.BlockSpec((None, tq, ATT_W), lambda b, h, i: (h, b * nq + i, 0)), kv, kv,
                  pl.BlockSpec((None, 1, LANES), lambda b, h, i: (layer, 0, g_off + h))],
        out_specs=pl.BlockSpec((tq, LANES), lambda b, h, i: (b * nq + i, h)),
        out_shape=jax.ShapeDtypeStruct((t, heads * LANES), BF16),
        scratch_shapes=[pltpu.VMEM((tq, LANES), F32), pltpu.VMEM((tq, ATT_W), F32)],
        compiler_params=_cparams(("parallel", "parallel", "arbitrary")),
        name=name,
    )(q, k, v, gain)


def _rope128(x, tab):
    c, s1, s2 = tab[:, :LANES], tab[:, LANES:2 * LANES], tab[:, 2 * LANES:]
    half = MLA_ROPE // 2
    return x * c + pltpu.roll(x, LANES - half, axis=1) * s1 + pltpu.roll(x, half, axis=1) * s2


def _mla_prep_kernel(ckv_ref, cq_ref, misc_ref, ang_ref, gq_ref, gkv_ref,
                     cqn_ref, ckvn_ref, krot_ref, tab_ref):
    cqn_ref[...] = _rms(cq_ref[...], gq_ref[...]).astype(cqn_ref.dtype)
    ckvn_ref[...] = _rms(ckv_ref[...], gkv_ref[...]).astype(ckvn_ref.dtype)
    ang = ang_ref[...]
    cos, sin = jnp.cos(ang), jnp.sin(ang)
    lane = lax.broadcasted_iota(jnp.int32, ang.shape, 1)
    half = MLA_ROPE // 2
    tab = jnp.concatenate([
        jnp.where(lane < MLA_ROPE, cos, 0.0),
        jnp.where(lane < half, -sin, 0.0),
        jnp.where((lane >= half) & (lane < MLA_ROPE), sin, 0.0)], axis=1)
    tab_ref[...] = tab
    krot_ref[...] = _rope128(misc_ref[...], tab).astype(krot_ref.dtype)


def _mla_prep(proj32, ang, gq, gkv, layer, *, tm=512):
    t = proj32.shape[0]
    tm = min(tm, t)
    return pl.pallas_call(
        _mla_prep_kernel,
        grid=(t // tm,),
        in_specs=[pl.BlockSpec((tm, MLA_KV_LORA), lambda i: (i, 0)),
                  pl.BlockSpec((tm, MLA_Q_LORA), lambda i: (i, 2)),
                  pl.BlockSpec((tm, LANES), lambda i: (i, P32_MISC)),
                  pl.BlockSpec((tm, LANES), lambda i: (i, 0)),
                  pl.BlockSpec((None, 1, MLA_Q_LORA), lambda i: (layer, 0, 0)),
                  pl.BlockSpec((None, 1, MLA_KV_LORA), lambda i: (layer, 0, 0))],
        out_specs=[pl.BlockSpec((tm, MLA_Q_LORA), lambda i: (i, 0)),
                   pl.BlockSpec((tm, MLA_KV_LORA), lambda i: (i, 0)),
                   pl.BlockSpec((tm, LANES), lambda i: (i, 0)),
                   pl.BlockSpec((tm, 3 * LANES), lambda i: (i, 0))],
        out_shape=[jax.ShapeDtypeStruct((t, MLA_Q_LORA), BF16),
                   jax.ShapeDtypeStruct((t, MLA_KV_LORA), BF16),
                   jax.ShapeDtypeStruct((t, LANES), BF16),
                   jax.ShapeDtypeStruct((t, 3 * LANES), F32)],
        compiler_params=_cparams(("parallel",)),
        name="mla_prep",
    )(proj32, proj32, proj32, ang, gq, gkv)


def _mla_proj_kernel(cqn_ref, ckvn_ref, krot_ref, tab_ref, wq_ref, wkv_ref, q_ref, k_ref, v_ref):
    q = jnp.dot(cqn_ref[...], wq_ref[...], preferred_element_type=F32)
    q_ref[:, :LANES] = q[:, :LANES].astype(q_ref.dtype)
    q_ref[:, LANES:] = _rope128(q[:, LANES:], tab_ref[...]).astype(q_ref.dtype)
    kv = jnp.dot(ckvn_ref[...], wkv_ref[...], preferred_element_type=F32)
    k_ref[:, :LANES] = kv[:, :LANES].astype(k_ref.dtype)
    k_ref[:, LANES:] = krot_ref[...]
    v_ref[:, :LANES] = kv[:, LANES:].astype(v_ref.dtype)
    v_ref[:, LANES:] = jnp.ones((v_ref.shape[0], LANES), v_ref.dtype)


def _mla_proj(cqn, ckvn, krot, tab, wq, wkv, layer, *, tm=2048):
    t = cqn.shape[0]
    tm = min(tm, t)
    tok = lambda w: pl.BlockSpec((tm, w), lambda i, h: (i, 0))
    out = pl.BlockSpec((None, tm, ATT_W), lambda i, h: (h, i, 0))
    shp = jax.ShapeDtypeStruct((MLA_HEADS, t, ATT_W), BF16)
    return pl.pallas_call(
        _mla_proj_kernel,
        grid=(t // tm, MLA_HEADS),
        in_specs=[tok(MLA_Q_LORA), tok(MLA_KV_LORA), tok(LANES), tok(3 * LANES),
                  pl.BlockSpec((None, None, MLA_Q_LORA, ATT_W), lambda i, h: (layer, h, 0, 0)),
                  pl.BlockSpec((None, None, MLA_KV_LORA, 2 * LANES), lambda i, h: (layer, h, 0, 0))],
        out_specs=[out, out, out],
        out_shape=[shp, shp, shp],
        compiler_params=_cparams(("parallel", "parallel")),
        name="mla_proj",
    )(cqn, ckvn, krot, tab, wq, wkv)


def _mem_attn_kernel(q_ref, k_ref, v_ref, o_ref):
    scale = MEM_DIM ** -0.5
    outs = []
    for h in range(MEM_HEADS):
        sl = slice(h * MEM_DIM, (h + 1) * MEM_DIM)
        s = lax.dot_general(q_ref[:, sl], k_ref[:, sl], (((1,), (1,)), ((), ())),
                            preferred_element_type=F32) * scale
        p = jnp.exp(s - jnp.max(s, axis=-1, keepdims=True))
        p = p / jnp.sum(p, axis=-1, keepdims=True)
        outs.append(jnp.dot(p.astype(BF16), v_ref[:, sl], preferred_element_type=F32))
    o_ref[...] = jnp.concatenate(outs, axis=1).astype(o_ref.dtype)


def _mem_attn(q, km, vm, *, batch, seq, mem_tokens, tm=512):
    tm = min(tm, seq)
    nt = seq // tm
    w = MEM_HEADS * MEM_DIM
    kv = pl.BlockSpec((mem_tokens, w), lambda i: (i // nt, 0))
    return pl.pallas_call(
        _mem_attn_kernel,
        grid=(batch * nt,),
        in_specs=[pl.BlockSpec((tm, w), lambda i: (i, 0)), kv, kv],
        out_specs=pl.BlockSpec((tm, w), lambda i: (i, 0)),
        out_shape=jax.ShapeDtypeStruct((batch * seq, w), BF16),
        compiler_params=_cparams(("parallel",)),
        name="mem_attn",
    )(q, km, vm)


def _split_w_in(w_in):
    sizes = (1024, 1024, 1024, 1024, 1024, 1024, 1024, FOX_HEADS, MLA_Q_LORA, MLA_KV_LORA, MLA_ROPE)
    offs = [0]
    for s in sizes:
        offs.append(offs[-1] + s)
    hq, hf, hi, hg, fq, fk, fv, ff, cq, ckv, kr = (w_in[:, :, offs[n]:offs[n + 1]] for n in range(len(sizes)))
    fq = fq * (FOX_DIM ** -0.5 * LOG2E)
    w16 = jnp.concatenate([hq, hi, hg, fq, fk, fv], axis=-1).astype(BF16)
    pad = jnp.zeros(w_in.shape[:2] + (2 * LANES - MLA_ROPE - FOX_HEADS,), w_in.dtype)
    w32 = jnp.concatenate([ckv, hf, cq, kr, ff, pad], axis=-1).astype(BF16)
    return w16, w32


def _split_mla_weights(w_uq, w_ukv):
    depth = w_uq.shape[0]
    wq = w_uq.reshape(depth, MLA_Q_LORA, MLA_HEADS, MLA_NOPE + MLA_ROPE).transpose(0, 2, 1, 3)
    wq = wq * ((MLA_NOPE + MLA_ROPE) ** -0.5 * LOG2E)
    wq = jnp.pad(wq, ((0, 0), (0, 0), (0, 0), (0, ATT_W - MLA_NOPE - MLA_ROPE)))
    wkv = w_ukv.reshape(depth, MLA_KV_LORA, MLA_HEADS, MLA_NOPE + MLA_V).transpose(0, 2, 1, 3)
    return wq.astype(BF16), wkv.astype(BF16)


def kernel(x, mem, positions, w_in, hg_lb_logits, fox_f_bias, mla_q_norm_g, mla_kv_norm_g, w_uq, w_ukv,
           mix_out_g, w_o, mem_norm_g, w_mq, w_mk, w_mv, w_mo, w_ff1, w_ff2, pre_mix_g, post_mix_g,
           pre_mem_g, post_mem_g, pre_ffn_g, post_ffn_g):
    batch, seq, d = x.shape
    depth = w_in.shape[0]
    t = batch * seq
    mem_tokens = mem.shape[1]

    w16, w32 = _split_w_in(w_in)
    wq, wkv = _split_mla_weights(w_uq, w_ukv)
    w_o16, w_mq16, w_mk16, w_mv16, w_mo16 = (w.astype(BF16) for w in (w_o, w_mq, w_mk, w_mv, w_mo))
    w_ff1_16, w_ff2_16 = w_ff1.astype(BF16), w_ff2.astype(BF16)
    row = lambda g: g.astype(F32).reshape(depth, 1, g.shape[-1])
    pre_mix, post_mix, pre_mem, post_mem, pre_ffn, post_ffn = map(
        row, (pre_mix_g, post_mix_g, pre_mem_g, post_mem_g, pre_ffn_g, post_ffn_g))
    mix_g, mem_g, gq, gkv = map(row, (mix_out_g, mem_norm_g, mla_q_norm_g, mla_kv_norm_g))
    fox_bias = jnp.pad(fox_f_bias.astype(F32), ((0, 0), (FF_LANE, LANES - FF_LANE - FOX_HEADS)))
    fox_bias = fox_bias.reshape(depth, 1, LANES)
    llb, l1m = _lower_bounds(hg_lb_logits)
    llb, l1m = row(llb), row(l1m)

    half = MLA_ROPE // 2
    inv = ROPE_THETA ** (-jnp.arange(half, dtype=F32) / half)
    ang = positions.reshape(t, 1).astype(F32) * inv[None, :]
    ang = jnp.concatenate([ang, ang, jnp.zeros((t, LANES - MLA_ROPE), F32)], axis=1)

    h = x.reshape(t, d)
    mem2 = mem.reshape(batch * mem_tokens, d)
    hn = _rms_cast(h, pre_mix, 0, name="rms_first")

    for l in range(depth):
        proj16 = _matmul(hn, w16, l, BF16, tm=1024, tn=1024, name="in_proj16")
        proj32 = _matmul(hn, w32, l, F32, tm=1024, tn=P32_COLS // 2, name="in_proj32")

        mix_a = _hgrn(proj16, proj32, llb, l1m, mix_g, l, batch=batch, seq=seq)

        fq, fk, fv = _fox_prep(proj16, proj32, fox_bias, l, batch=batch, seq=seq)
        mix_b = _flash(fq, fk, fv, mix_g, l, HG_HEADS, batch=batch, seq=seq, name="fox_attn")

        cqn, ckvn, krot, tab = _mla_prep(proj32, ang, gq, gkv, l)
        q_c, k_c, v_c = _mla_proj(cqn, ckvn, krot, tab, wq, wkv, l)
        mix_c = _flash(q_c, k_c, v_c, mix_g, l, HG_HEADS + FOX_HEADS, batch=batch, seq=seq, name="mla_attn")

        y = _matmul_parts([mix_a, mix_b, mix_c], w_o16, l, F32, tm=1024, tn=1024, name="mix_out")
        h, hn = _resid_norm(h, y, post_mix, l, pre_mem, l)

        mem_n = _rms_cast(mem2, mem_g, l, name="rms_mem")
        km = _matmul(mem_n, w_mk16, l, BF16, tm=512, tn=512, name="mem_k")
        vm = _matmul(mem_n, w_mv16, l, BF16, tm=512, tn=512, name="mem_v")
        qm = _matmul(hn, w_mq16, l, BF16, tm=1024, tn=512, name="mem_q")
        om = _mem_attn(qm, km, vm, batch=batch, seq=seq, mem_tokens=mem_tokens)
        y = _matmul(om, w_mo16, l, F32, tm=1024, tn=1024, name="mem_out")
        h, hn = _resid_norm(h, y, post_mem, l, pre_ffn, l)

        u = _matmul(hn, w_ff1_16, l, BF16, tm=1024, tn=1024, relu2=True, name="ffn_up")
        y = _matmul(u, w_ff2_16, l, F32, tm=1024, tn=1024, tk=4096, name="ffn_down")
        if l + 1 < depth:
            h, hn = _resid_norm(h, y, post_ffn, l, pre_mix, l + 1)
        else:
            h, _ = _resid_norm(h, y, post_ffn, l)

    return h.reshape(batch, seq, d)
```

```python
import functools

import jax
import jax.numpy as jnp
from jax import lax
from jax.experimental import pallas as pl
from jax.experimental.pallas import tpu as pltpu

F32 = jnp.float32
BF16 = jnp.bfloat16

D_MODEL = 4096
DEPTH = 4
HG_HEADS = 8
HG_DK = 128
HG_DV = 128
FOX_HEADS = 8
FOX_DIM = 128
MLA_HEADS = 16
MLA_Q_LORA = 768
MLA_KV_LORA = 512
MLA_NOPE = 128
MLA_ROPE = 64
MLA_V = 128
ROPE_THETA = 10000.0
MEM_HEADS = 4
MEM_DIM = 128
EPS = 1e-6
MASK_VALUE = -1e30
LB_FLOOR = 1e-30
LOG2E = 1.4426950408889634

LANES = 128
SUBLANES = 8
HG_CHUNK = 64
HG_SUB = 16
HG_UNROLL = 8
ATT_W = 2 * LANES
FLASH_TQ = 2048
FLASH_SUB = 512
FLASH_HP = 2
FLASH_KV_PER_TRIP = 1

P16_HQ, P16_HI, P16_HG = 0, 8, 16
P16_FQKV = 3
P16_COLS = 6 * 1024
P32_HF = 4
P32_MISC = 18
P32_COLS = 20 * LANES
FF_LANE = 64


def _cparams(sem, vmem_mb=None):
    kw = dict(dimension_semantics=sem)
    if vmem_mb is not None:
        kw["vmem_limit_bytes"] = vmem_mb * 1024 * 1024
    return pltpu.CompilerParams(**kw)


def _mm_kernel(a_ref, b_ref, o_ref, *scratch, nk, relu2):
    def finish(r):
        if relu2:
            r = jnp.square(jnp.maximum(r, 0.0))
        o_ref[...] = r.astype(o_ref.dtype)

    if nk == 1:
        finish(jnp.dot(a_ref[...], b_ref[...], preferred_element_type=F32))
        return
    (acc_ref,) = scratch
    k = pl.program_id(2)

    @pl.when(k == 0)
    def _():
        acc_ref[...] = jnp.zeros_like(acc_ref)

    acc_ref[...] += jnp.dot(a_ref[...], b_ref[...], preferred_element_type=F32)

    @pl.when(k == nk - 1)
    def _():
        finish(acc_ref[...])


def _matmul(a, w, layer, out_dtype, *, tm, tn, tk=None, relu2=False, name="mm"):
    m, kdim = a.shape
    n = w.shape[-1]
    tk = kdim if tk is None else min(tk, kdim)
    tm, tn = min(tm, m), min(tn, n)
    assert m % tm == 0 and n % tn == 0 and kdim % tk == 0
    nk = kdim // tk
    scratch = [] if nk == 1 else [pltpu.VMEM((tm, tn), F32)]
    return pl.pallas_call(
        functools.partial(_mm_kernel, nk=nk, relu2=relu2),
        grid=(m // tm, n // tn, nk),
        in_specs=[
            pl.BlockSpec((tm, tk), lambda i, j, k: (i, k)),
            pl.BlockSpec((None, tk, tn), lambda i, j, k: (layer, k, j)),
        ],
        out_specs=pl.BlockSpec((tm, tn), lambda i, j, k: (i, j)),
        out_shape=jax.ShapeDtypeStruct((m, n), out_dtype),
        scratch_shapes=scratch,
        compiler_params=_cparams(("parallel", "parallel", "arbitrary"), 56),
        name=name,
    )(a, w)


def _mm_parts_kernel(*refs, widths):
    a_refs, b_ref, o_ref = refs[:len(widths)], refs[len(widths)], refs[len(widths) + 1]
    acc, off = None, 0
    for a_ref, wd in zip(a_refs, widths):
        r = jnp.dot(a_ref[...], b_ref[off:off + wd, :], preferred_element_type=F32)
        acc = r if acc is None else acc + r
        off += wd
    o_ref[...] = acc.astype(o_ref.dtype)


def _matmul_parts(parts, w, layer, out_dtype, *, tm, tn, name):
    m = parts[0].shape[0]
    widths = tuple(p.shape[1] for p in parts)
    kdim, n = sum(widths), w.shape[-1]
    tm, tn = min(tm, m), min(tn, n)
    assert m % tm == 0 and n % tn == 0 and w.shape[-2] == kdim
    return pl.pallas_call(
        functools.partial(_mm_parts_kernel, widths=widths),
        grid=(m // tm, n // tn),
        in_specs=[pl.BlockSpec((tm, wd), lambda i, j: (i, 0)) for wd in widths]
        + [pl.BlockSpec((None, kdim, tn), lambda i, j: (layer, 0, j))],
        out_specs=pl.BlockSpec((tm, tn), lambda i, j: (i, j)),
        out_shape=jax.ShapeDtypeStruct((m, n), out_dtype),
        compiler_params=_cparams(("parallel", "parallel"), 56),
        name=name,
    )(*parts, w)


def _rms(x, g):
    return x * lax.rsqrt(jnp.mean(x * x, axis=-1, keepdims=True) + EPS) * g


def _rms_cast_kernel(x_ref, g_ref, o_ref):
    o_ref[...] = _rms(x_ref[...].astype(F32), g_ref[...]).astype(o_ref.dtype)


def _rms_cast(x, g, layer, *, tm=256, name="rms_cast"):
    m, d = x.shape
    tm = min(tm, m)
    return pl.pallas_call(
        _rms_cast_kernel,
        grid=(m // tm,),
        in_specs=[pl.BlockSpec((tm, d), lambda i: (i, 0)),
                  pl.BlockSpec((None, 1, d), lambda i: (layer, 0, 0))],
        out_specs=pl.BlockSpec((tm, d), lambda i: (i, 0)),
        out_shape=jax.ShapeDtypeStruct((m, d), BF16),
        compiler_params=_cparams(("parallel",)),
        name=name,
    )(x, g)


def _resid_norm_kernel(h_ref, y_ref, gp_ref, *rest, with_next):
    h_new = h_ref[...] + _rms(y_ref[...].astype(F32), gp_ref[...])
    if with_next:
        gn_ref, ho_ref, hn_ref = rest
        hn_ref[...] = _rms(h_new, gn_ref[...]).astype(hn_ref.dtype)
    else:
        (ho_ref,) = rest
    ho_ref[...] = h_new


def _resid_norm(h, y, g_post, layer, g_next=None, layer_next=None, *, tm=256):
    m, d = h.shape
    tm = min(tm, m)
    row = pl.BlockSpec((tm, d), lambda i: (i, 0))
    with_next = g_next is not None
    in_specs = [row, row, pl.BlockSpec((None, 1, d), lambda i: (layer, 0, 0))]
    args = [h, y, g_post]
    out_shape = [jax.ShapeDtypeStruct((m, d), F32)]
    out_specs = [row]
    if with_next:
        in_specs.append(pl.BlockSpec((None, 1, d), lambda i: (layer_next, 0, 0)))
        args.append(g_next)
        out_shape.append(jax.ShapeDtypeStruct((m, d), BF16))
        out_specs.append(row)
    res = pl.pallas_call(
        functools.partial(_resid_norm_kernel, with_next=with_next),
        grid=(m // tm,),
        in_specs=in_specs,
        out_specs=out_specs,
        out_shape=out_shape,
        compiler_params=_cparams(("parallel",)),
        name="resid_norm",
    )(*args)
    return (res[0], res[1]) if with_next else (res[0], None)


def _lower_bound_kernel(x_ref, llb_ref, l1m_ref):
    depth = x_ref.shape[0]
    rows = [x_ref[i:i + 1, :].astype(F32) for i in range(depth)]
    mx = functools.reduce(jnp.maximum, rows)
    ex = [jnp.exp(r - mx) for r in rows]
    tot = functools.reduce(lambda a, b: a + b, ex)
    p = [e / tot for e in ex]
    cum = p[0]
    for i in range(depth):
        if i > 0:
            cum = cum + p[i]
        lb = jnp.clip(cum - p[0], 0.0, 1.0 - 1e-6)
        llb_ref[i:i + 1, :] = jnp.log(jnp.maximum(lb, LB_FLOOR))
        l1m_ref[i:i + 1, :] = jnp.log1p(-lb)


def _lower_bounds(logits):
    shp = jax.ShapeDtypeStruct(logits.shape, F32)
    return pl.pallas_call(_lower_bound_kernel, out_shape=[shp, shp], name="hgrn_lower_bounds")(logits)


def _hgrn_kernel(q_ref, v_ref, gate_ref, z_ref, llb_ref, l1m_ref, g_ref, o_ref,
                 state_ref, b_scr, k_scr, v_scr, *, n_chunks):
    c_len = HG_CHUNK
    n_rows = c_len // SUBLANES

    @pl.when(pl.program_id(2) == 0)
    def _():
        state_ref[...] = jnp.zeros_like(state_ref)

    llb = llb_ref[...]
    l1m = l1m_ref[...]
    gain = g_ref[...]
    r_i = lax.broadcasted_iota(jnp.int32, (c_len, c_len), 0)
    c_i = lax.broadcasted_iota(jnp.int32, (c_len, c_len), 1)
    tri = (c_i <= r_i).astype(F32)
    sub = lax.broadcasted_iota(jnp.int32, (SUBLANES, LANES), 0)
    takes = [(c_i >= jb * HG_SUB) & (c_i < (jb + 1) * HG_SUB) & (r_i >= (jb + 1) * HG_SUB)
             for jb in range(c_len // HG_SUB - 1)]

    def chunk(c, slot, state_t):
        rows = pl.ds(pl.multiple_of(c * c_len, c_len), c_len)
        z = z_ref[rows, :]
        q = q_ref[rows, :].astype(F32)
        v16 = v_ref[rows, :]
        sp = jnp.log1p(jnp.exp(-jnp.abs(z)))
        ls = jnp.minimum(z, 0.0) - sp
        a2 = l1m + ls
        log_f = jnp.maximum(llb, a2) + jnp.log1p(jnp.exp(-jnp.abs(llb - a2)))
        kk = jnp.exp(l1m + (ls - z))
        b = jnp.dot(tri, log_f, preferred_element_type=F32, precision=lax.Precision.HIGHEST) * LOG2E
        b_scr[slot] = b
        k_scr[slot] = kk
        v_scr[slot] = v16.astype(F32)

        q_rows = [q[r * SUBLANES:(r + 1) * SUBLANES, :] for r in range(n_rows)]
        b_rows = [b[r * SUBLANES:(r + 1) * SUBLANES, :] for r in range(n_rows)]
        o_rows = [jnp.zeros((SUBLANES, LANES), F32) for _ in range(n_rows)]
        rows_per_sb = HG_SUB // SUBLANES
        for g in range(n_rows):
            r_hi = (g // rows_per_sb + 1) * rows_per_sb
            for i in range(SUBLANES):
                s = g * SUBLANES + i
                bs = b_scr[slot, s:s + 1, :]
                ks = k_scr[slot, s:s + 1, :]
                vs = v_scr[slot, s:s + 1, :]
                for r in range(g, r_hi):
                    e = jnp.exp2(b_rows[r] - bs)
                    if r == g and i > 0:
                        e = jnp.where(sub >= i, e, 0.0)
                    p = e * (q_rows[r] * ks)
                    o_rows[r] = o_rows[r] + jnp.sum(p, axis=-1, keepdims=True) * vs
        o = jnp.concatenate(o_rows, axis=0)

        n_sb = c_len // HG_SUB
        bend = [b[(jb + 1) * HG_SUB - 1:(jb + 1) * HG_SUB, :] for jb in range(n_sb)]
        bend_rows = jnp.concatenate([jnp.broadcast_to(e_, (HG_SUB, LANES)) for e_ in bend], axis=0)
        khat = (kk * jnp.exp2(bend_rows - b)).astype(BF16)
        a_stack = jnp.concatenate(
            [q * jnp.exp2(jnp.minimum(b - bend[jb], 0.0)) for jb in range(n_sb - 1)], axis=0).astype(BF16)
        r_all = lax.dot_general(a_stack, khat, (((1,), (1,)), ((), ())), preferred_element_type=F32)
        s_off = jnp.zeros((c_len, c_len), F32)
        for jb in range(n_sb - 1):
            s_off = jnp.where(takes[jb], r_all[jb * c_len:(jb + 1) * c_len, :], s_off)
        o = o + jnp.dot(s_off.astype(BF16), v16, preferred_element_type=F32)

        qe = (q * jnp.exp2(b)).astype(BF16)
        o = o + lax.dot_general(qe, state_t.astype(BF16), (((1,), (1,)), ((), ())),
                                preferred_element_type=F32)
        b_end = b[c_len - 1:c_len, :]
        kd = (kk * jnp.exp2(b_end - b)).astype(BF16)
        upd = lax.dot_general(v16, kd, (((0,), (0,)), ((), ())), preferred_element_type=F32)
        state_t = state_t * jnp.exp2(b_end) + upd

        gate = gate_ref[rows, :].astype(F32)
        y = _rms(o, gain) * (gate * jax.nn.sigmoid(gate))
        o_ref[rows, :] = y.astype(o_ref.dtype)
        return state_t

    def trip(cc, state_t):
        for slot in range(HG_UNROLL):
            state_t = chunk(HG_UNROLL * cc + slot, slot, state_t)
        return state_t

    state_ref[...] = lax.fori_loop(0, n_chunks // HG_UNROLL, trip, state_ref[...])


def _hgrn(proj16, proj32, llb, l1m, mix_g, layer, *, batch, seq, blk=512):
    blk = min(blk, seq)
    ns = seq // blk
    assert (blk // HG_CHUNK) % HG_UNROLL == 0
    tok = lambda col0: pl.BlockSpec((blk, LANES), lambda b, h, s: (b * ns + s, col0 + h))
    par = pl.BlockSpec((None, 1, LANES), lambda b, h, s: (layer, 0, h))
    return pl.pallas_call(
        functools.partial(_hgrn_kernel, n_chunks=blk // HG_CHUNK),
        grid=(batch, HG_HEADS, ns),
        in_specs=[tok(P16_HQ), tok(P16_HI), tok(P16_HG), tok(P32_HF), par, par, par],
        out_specs=pl.BlockSpec((blk, LANES), lambda b, h, s: (b * ns + s, h)),
        out_shape=jax.ShapeDtypeStruct((batch * seq, HG_HEADS * HG_DV), BF16),
        scratch_shapes=[pltpu.VMEM((HG_DV, HG_DK), F32)] + [pltpu.VMEM((HG_UNROLL, HG_CHUNK, LANES), F32)] * 3,
        compiler_params=_cparams(("parallel", "parallel", "arbitrary")),
        name="hgrn2_scan",
    )(proj16, proj16, proj16, proj32, llb, l1m, mix_g)


def _fox_prep_kernel(x_ref, bias_ref, fq_ref, fk_ref, fv_ref, qo_ref, ko_ref, vo_ref, carry_ref, *, blk):
    @pl.when(pl.program_id(1) == 0)
    def _():
        carry_ref[...] = jnp.zeros_like(carry_ref)

    x = x_ref[...] + bias_ref[...]
    lf = jnp.minimum(x, 0.0) - jnp.log1p(jnp.exp(-jnp.abs(x)))
    r_i = lax.broadcasted_iota(jnp.int32, (blk, blk), 0)
    c_i = lax.broadcasted_iota(jnp.int32, (blk, blk), 1)
    tri = (c_i <= r_i).astype(F32)
    cum = jnp.dot(tri, lf, preferred_element_type=F32, precision=lax.Precision.HIGHEST) + carry_ref[...]
    carry_ref[...] = cum[blk - 1:blk, :]
    c2 = cum * LOG2E
    lane = lax.broadcasted_iota(jnp.int32, (blk, LANES), 1)
    ones = jnp.ones((blk, LANES), BF16)
    for h in range(FOX_HEADS):
        col = jnp.broadcast_to(c2[:, FF_LANE + h:FF_LANE + h + 1], (blk, LANES))
        hi = col.astype(BF16).astype(F32)
        r1 = col - hi
        mid = r1.astype(BF16).astype(F32)
        lo = r1 - mid
        pieces = jnp.where((lane == 0) | (lane == 3), hi, jnp.where((lane == 1) | (lane == 4), mid, lo))
        q_ext = jnp.where(lane < 3, pieces, jnp.where(lane < 6, 1.0, 0.0))
        k_ext = jnp.where(lane < 3, 1.0, jnp.where(lane < 6, -pieces, 0.0))
        sl = slice(h * LANES, (h + 1) * LANES)
        qo_ref[h, :, :LANES] = fq_ref[:, sl]
        qo_ref[h, :, LANES:] = q_ext.astype(BF16)
        ko_ref[h, :, :LANES] = fk_ref[:, sl]
        ko_ref[h, :, LANES:] = k_ext.astype(BF16)
        vo_ref[h, :, :LANES] = fv_ref[:, sl]
        vo_ref[h, :, LANES:] = ones


def _fox_prep(proj16, proj32, bias_row, layer, *, batch, seq, blk=256):
    blk = min(blk, seq)
    nb = seq // blk
    t = batch * seq
    w = FOX_HEADS * FOX_DIM
    tok = lambda c: pl.BlockSpec((blk, w), lambda b, j: (b * nb + j, P16_FQKV + c))
    out = pl.BlockSpec((FOX_HEADS, blk, ATT_W), lambda b, j: (0, b * nb + j, 0))
    shp = jax.ShapeDtypeStruct((FOX_HEADS, t, ATT_W), BF16)
    return pl.pallas_call(
        functools.partial(_fox_prep_kernel, blk=blk),
        grid=(batch, nb),
        in_specs=[pl.BlockSpec((blk, LANES), lambda b, j: (b * nb + j, P32_MISC)),
                  pl.BlockSpec((None, 1, LANES), lambda b, j: (layer, 0, 0)),
                  tok(0), tok(1), tok(2)],
        out_specs=[out, out, out],
        out_shape=[shp, shp, shp],
        scratch_shapes=[pltpu.VMEM((1, LANES), F32)],
        compiler_params=_cparams(("parallel", "arbitrary")),
        name="fox_prep",
    )(proj32, bias_row, proj16, proj16, proj16)


def _flash_kernel(q_ref, k_ref, v_ref, g_ref, o_ref, m_scr, acc_scr, *, tq, sub, hp):
    i = pl.program_id(2)
    n_sub = tq // sub
    m_scr[...] = jnp.full_like(m_scr, MASK_VALUE)
    acc_scr[...] = jnp.zeros_like(acc_scr)
    r_i = lax.broadcasted_iota(jnp.int32, (sub, sub), 0)
    c_i = lax.broadcasted_iota(jnp.int32, (sub, sub), 1)

    def step(hd, r, kv0, diagonal):
        rows = slice(r * sub, (r + 1) * sub)
        keys = pl.ds(kv0, sub)
        s = lax.dot_general(q_ref[hd, rows, :], k_ref[hd, keys, :], (((1,), (1,)), ((), ())),
                            preferred_element_type=F32)
        if diagonal:
            s = jnp.where(c_i <= r_i, s, MASK_VALUE)
        m_prev = m_scr[hd, rows, :]
        m_new = jnp.maximum(m_prev, jnp.max(s, axis=-1, keepdims=True))
        alpha = jnp.exp2(m_prev - m_new)
        p = jnp.exp2(s - jnp.concatenate([m_new] * (sub // LANES), axis=1)).astype(BF16)
        pv = jnp.dot(p, v_ref[hd, keys, :], preferred_element_type=F32)
        acc_scr[hd, rows, :] = jnp.concatenate([alpha, alpha], axis=1) * acc_scr[hd, rows, :] + pv
        m_scr[hd, rows, :] = m_new

    def full_chunks(j, carry):
        kv0 = pl.multiple_of(j * (FLASH_KV_PER_TRIP * sub), FLASH_KV_PER_TRIP * sub)
        for jj in range(FLASH_KV_PER_TRIP):
            for hd in range(hp):
                for r in range(n_sub):
                    step(hd, r, kv0 + jj * sub, False)
        return carry

    assert n_sub % FLASH_KV_PER_TRIP == 0
    lax.fori_loop(0, i * n_sub // FLASH_KV_PER_TRIP, full_chunks, 0)
    base = pl.multiple_of(i * tq, tq)
    for jj in range(n_sub):
        for hd in range(hp):
            for r in range(jj, n_sub):
                step(hd, r, base + jj * sub, jj == r)

    for hd in range(hp):
        acc = acc_scr[hd]
        o = acc[:, :LANES] / acc[:, LANES:]
        cols = slice(hd * LANES, (hd + 1) * LANES)
        o_ref[:, cols] = _rms(o, g_ref[:, cols]).astype(o_ref.dtype)


def _flash(q, k, v, gain, layer, g_off, *, batch, seq, name):
    heads, t, _ = q.shape
    tq, sub, hp = min(FLASH_TQ, seq), min(FLASH_SUB, seq), FLASH_HP
    nq = seq // tq
    assert heads % hp == 0 and g_off % hp == 0
    kv = pl.BlockSpec((hp, seq, ATT_W), lambda b, h, i: (h, b, 0))
    return pl.pallas_call(
        functools.partial(_flash_kernel, tq=tq, sub=sub, hp=hp),
        grid=(batch, heads // hp, nq),
        in_specs=[pl.BlockSpec((hp, tq, ATT_W), lambda b, h, i: (h, b * nq + i, 0)), kv, kv,
                  pl.BlockSpec((None, 1, hp * LANES), lambda b, h, i: (layer, 0, g_off // hp + h))],
        out_specs=pl.BlockSpec((tq, hp * LANES), lambda b, h, i: (b * nq + i, h)),
        out_shape=jax.ShapeDtypeStruct((t, heads * LANES), BF16),
        scratch_shapes=[pltpu.VMEM((hp, tq, LANES), F32), pltpu.VMEM((hp, tq, ATT_W), F32)],
        compiler_params=_cparams(("parallel", "parallel", "arbitrary"), 56),
        name=name,
    )(q, k, v, gain)


def _rope128(x, tab):
    c, s1, s2 = tab[:, :LANES], tab[:, LANES:2 * LANES], tab[:, 2 * LANES:]
    half = MLA_ROPE // 2
    return x * c + pltpu.roll(x, LANES - half, axis=1) * s1 + pltpu.roll(x, half, axis=1) * s2


def _mla_prep_kernel(ckv_ref, cq_ref, misc_ref, ang_ref, gq_ref, gkv_ref,
                     cqn_ref, ckvn_ref, krot_ref, tab_ref):
    cqn_ref[...] = _rms(cq_ref[...], gq_ref[...]).astype(cqn_ref.dtype)
    ckvn_ref[...] = _rms(ckv_ref[...], gkv_ref[...]).astype(ckvn_ref.dtype)
    ang = ang_ref[...]
    cos, sin = jnp.cos(ang), jnp.sin(ang)
    lane = lax.broadcasted_iota(jnp.int32, ang.shape, 1)
    half = MLA_ROPE // 2
    tab = jnp.concatenate([
        jnp.where(lane < MLA_ROPE, cos, 0.0),
        jnp.where(lane < half, -sin, 0.0),
        jnp.where((lane >= half) & (lane < MLA_ROPE), sin, 0.0)], axis=1)
    tab_ref[...] = tab
    krot_ref[...] = _rope128(misc_ref[...], tab).astype(krot_ref.dtype)


def _mla_prep(proj32, ang, gq, gkv, layer, *, tm=512):
    t = proj32.shape[0]
    tm = min(tm, t)
    return pl.pallas_call(
        _mla_prep_kernel,
        grid=(t // tm,),
        in_specs=[pl.BlockSpec((tm, MLA_KV_LORA), lambda i: (i, 0)),
                  pl.BlockSpec((tm, MLA_Q_LORA), lambda i: (i, 2)),
                  pl.BlockSpec((tm, LANES), lambda i: (i, P32_MISC)),
                  pl.BlockSpec((tm, LANES), lambda i: (i, 0)),
                  pl.BlockSpec((None, 1, MLA_Q_LORA), lambda i: (layer, 0, 0)),
                  pl.BlockSpec((None, 1, MLA_KV_LORA), lambda i: (layer, 0, 0))],
        out_specs=[pl.BlockSpec((tm, MLA_Q_LORA), lambda i: (i, 0)),
                   pl.BlockSpec((tm, MLA_KV_LORA), lambda i: (i, 0)),
                   pl.BlockSpec((tm, LANES), lambda i: (i, 0)),
                   pl.BlockSpec((tm, 3 * LANES), lambda i: (i, 0))],
        out_shape=[jax.ShapeDtypeStruct((t, MLA_Q_LORA), BF16),
                   jax.ShapeDtypeStruct((t, MLA_KV_LORA), BF16),
                   jax.ShapeDtypeStruct((t, LANES), BF16),
                   jax.ShapeDtypeStruct((t, 3 * LANES), F32)],
        compiler_params=_cparams(("parallel",)),
        name="mla_prep",
    )(proj32, proj32, proj32, ang, gq, gkv)


def _mla_proj_kernel(cqn_ref, ckvn_ref, krot_ref, tab_ref, wq_ref, wkv_ref, q_ref, k_ref, v_ref):
    q = jnp.dot(cqn_ref[...], wq_ref[...], preferred_element_type=F32)
    q_ref[:, :LANES] = q[:, :LANES].astype(q_ref.dtype)
    q_ref[:, LANES:] = _rope128(q[:, LANES:], tab_ref[...]).astype(q_ref.dtype)
    kv = jnp.dot(ckvn_ref[...], wkv_ref[...], preferred_element_type=F32)
    k_ref[:, :LANES] = kv[:, :LANES].astype(k_ref.dtype)
    k_ref[:, LANES:] = krot_ref[...]
    v_ref[:, :LANES] = kv[:, LANES:].astype(v_ref.dtype)
    v_ref[:, LANES:] = jnp.ones((v_ref.shape[0], LANES), v_ref.dtype)


def _mla_proj(cqn, ckvn, krot, tab, wq, wkv, layer, *, tm=2048):
    t = cqn.shape[0]
    tm = min(tm, t)
    tok = lambda w: pl.BlockSpec((tm, w), lambda i, h: (i, 0))
    out = pl.BlockSpec((None, tm, ATT_W), lambda i, h: (h, i, 0))
    shp = jax.ShapeDtypeStruct((MLA_HEADS, t, ATT_W), BF16)
    return pl.pallas_call(
        _mla_proj_kernel,
        grid=(t // tm, MLA_HEADS),
        in_specs=[tok(MLA_Q_LORA), tok(MLA_KV_LORA), tok(LANES), tok(3 * LANES),
                  pl.BlockSpec((None, None, MLA_Q_LORA, ATT_W), lambda i, h: (layer, h, 0, 0)),
                  pl.BlockSpec((None, None, MLA_KV_LORA, 2 * LANES), lambda i, h: (layer, h, 0, 0))],
        out_specs=[out, out, out],
        out_shape=[shp, shp, shp],
        compiler_params=_cparams(("parallel", "parallel")),
        name="mla_proj",
    )(cqn, ckvn, krot, tab, wq, wkv)


def _mem_block_kernel(hn_ref, h_ref, wq_ref, k_ref, v_ref, wo_ref, gp_ref, gn_ref, ho_ref, hno_ref):
    scale = MEM_DIM ** -0.5
    q = jnp.dot(hn_ref[...], wq_ref[...], preferred_element_type=F32).astype(BF16)
    outs = []
    for h in range(MEM_HEADS):
        sl = slice(h * MEM_DIM, (h + 1) * MEM_DIM)
        s = lax.dot_general(q[:, sl], k_ref[:, sl], (((1,), (1,)), ((), ())),
                            preferred_element_type=F32) * scale
        p = jnp.exp(s - jnp.max(s, axis=-1, keepdims=True))
        p = p / jnp.sum(p, axis=-1, keepdims=True)
        outs.append(jnp.dot(p.astype(BF16), v_ref[:, sl], preferred_element_type=F32))
    o = jnp.concatenate(outs, axis=1).astype(BF16)
    y = jnp.dot(o, wo_ref[...], preferred_element_type=F32)
    h_new = h_ref[...] + _rms(y, gp_ref[...])
    ho_ref[...] = h_new
    hno_ref[...] = _rms(h_new, gn_ref[...]).astype(hno_ref.dtype)


def _mem_block(hn, h, w_mq, km, vm, w_mo, g_post, g_next, layer, *, seq, mem_tokens, tm=256):
    t, d = h.shape
    tm = min(tm, seq)
    nt = seq // tm
    w = MEM_HEADS * MEM_DIM
    row = pl.BlockSpec((tm, d), lambda i: (i, 0))
    kv = pl.BlockSpec((mem_tokens, w), lambda i: (i // nt, 0))
    gain = pl.BlockSpec((None, 1, d), lambda i: (layer, 0, 0))
    return pl.pallas_call(
        _mem_block_kernel,
        grid=(t // tm,),
        in_specs=[row, row, pl.BlockSpec((None, d, w), lambda i: (layer, 0, 0)), kv, kv,
                  pl.BlockSpec((None, w, d), lambda i: (layer, 0, 0)), gain, gain],
        out_specs=[row, row],
        out_shape=[jax.ShapeDtypeStruct((t, d), F32), jax.ShapeDtypeStruct((t, d), BF16)],
        compiler_params=_cparams(("parallel",), 56),
        name="mem_block",
    )(hn, h, w_mq, km, vm, w_mo, g_post, g_next)


def _split_w_in(w_in):
    sizes = (1024, 1024, 1024, 1024, 1024, 1024, 1024, FOX_HEADS, MLA_Q_LORA, MLA_KV_LORA, MLA_ROPE)
    offs = [0]
    for s in sizes:
        offs.append(offs[-1] + s)
    hq, hf, hi, hg, fq, fk, fv, ff, cq, ckv, kr = (w_in[:, :, offs[n]:offs[n + 1]] for n in range(len(sizes)))
    fq = fq * (FOX_DIM ** -0.5 * LOG2E)
    w16 = jnp.concatenate([hq, hi, hg, fq, fk, fv], axis=-1).astype(BF16)
    pad = jnp.zeros(w_in.shape[:2] + (2 * LANES - MLA_ROPE - FOX_HEADS,), w_in.dtype)
    w32 = jnp.concatenate([ckv, hf, cq, kr, ff, pad], axis=-1).astype(BF16)
    return w16, w32


def _split_mla_weights(w_uq, w_ukv):
    depth = w_uq.shape[0]
    wq = w_uq.reshape(depth, MLA_Q_LORA, MLA_HEADS, MLA_NOPE + MLA_ROPE).transpose(0, 2, 1, 3)
    wq = wq * ((MLA_NOPE + MLA_ROPE) ** -0.5 * LOG2E)
    wq = jnp.pad(wq, ((0, 0), (0, 0), (0, 0), (0, ATT_W - MLA_NOPE - MLA_ROPE)))
    wkv = w_ukv.reshape(depth, MLA_KV_LORA, MLA_HEADS, MLA_NOPE + MLA_V).transpose(0, 2, 1, 3)
    return wq.astype(BF16), wkv.astype(BF16)


def kernel(x, mem, positions, w_in, hg_lb_logits, fox_f_bias, mla_q_norm_g, mla_kv_norm_g, w_uq, w_ukv,
           mix_out_g, w_o, mem_norm_g, w_mq, w_mk, w_mv, w_mo, w_ff1, w_ff2, pre_mix_g, post_mix_g,
           pre_mem_g, post_mem_g, pre_ffn_g, post_ffn_g):
    batch, seq, d = x.shape
    depth = w_in.shape[0]
    t = batch * seq
    mem_tokens = mem.shape[1]

    w16, w32 = _split_w_in(w_in)
    wq, wkv = _split_mla_weights(w_uq, w_ukv)
    w_o16, w_mq16, w_mk16, w_mv16, w_mo16 = (w.astype(BF16) for w in (w_o, w_mq, w_mk, w_mv, w_mo))
    w_ff1_16, w_ff2_16 = w_ff1.astype(BF16), w_ff2.astype(BF16)
    row = lambda g: g.astype(F32).reshape(depth, 1, g.shape[-1])
    pre_mix, post_mix, pre_mem, post_mem, pre_ffn, post_ffn = map(
        row, (pre_mix_g, post_mix_g, pre_mem_g, post_mem_g, pre_ffn_g, post_ffn_g))
    mix_g, mem_g, gq, gkv = map(row, (mix_out_g, mem_norm_g, mla_q_norm_g, mla_kv_norm_g))
    fox_bias = jnp.pad(fox_f_bias.astype(F32), ((0, 0), (FF_LANE, LANES - FF_LANE - FOX_HEADS)))
    fox_bias = fox_bias.reshape(depth, 1, LANES)
    llb, l1m = _lower_bounds(hg_lb_logits)
    llb, l1m = row(llb), row(l1m)

    half = MLA_ROPE // 2
    inv = ROPE_THETA ** (-jnp.arange(half, dtype=F32) / half)
    ang = positions.reshape(t, 1).astype(F32) * inv[None, :]
    ang = jnp.concatenate([ang, ang, jnp.zeros((t, LANES - MLA_ROPE), F32)], axis=1)

    h = x.reshape(t, d)
    mem2 = mem.reshape(batch * mem_tokens, d)
    hn = _rms_cast(h, pre_mix, 0, name="rms_first")

    for l in range(depth):
        proj16 = _matmul(hn, w16, l, BF16, tm=1024, tn=1024, name="in_proj16")
        proj32 = _matmul(hn, w32, l, F32, tm=1024, tn=P32_COLS // 2, name="in_proj32")

        mix_a = _hgrn(proj16, proj32, llb, l1m, mix_g, l, batch=batch, seq=seq)

        fq, fk, fv = _fox_prep(proj16, proj32, fox_bias, l, batch=batch, seq=seq)
        mix_b = _flash(fq, fk, fv, mix_g, l, HG_HEADS, batch=batch, seq=seq, name="fox_attn")

        cqn, ckvn, krot, tab = _mla_prep(proj32, ang, gq, gkv, l)
        q_c, k_c, v_c = _mla_proj(cqn, ckvn, krot, tab, wq, wkv, l)
        mix_c = _flash(q_c, k_c, v_c, mix_g, l, HG_HEADS + FOX_HEADS, batch=batch, seq=seq, name="mla_attn")

        y = _matmul_parts([mix_a, mix_b, mix_c], w_o16, l, BF16, tm=1024, tn=1024, name="mix_out")
        h, hn = _resid_norm(h, y, post_mix, l, pre_mem, l)

        mem_n = _rms_cast(mem2, mem_g, l, name="rms_mem")
        km = _matmul(mem_n, w_mk16, l, BF16, tm=512, tn=512, name="mem_k")
        vm = _matmul(mem_n, w_mv16, l, BF16, tm=512, tn=512, name="mem_v")
        h, hn = _mem_block(hn, h, w_mq16, km, vm, w_mo16, post_mem, pre_ffn, l, seq=seq, mem_tokens=mem_tokens)

        u = _matmul(hn, w_ff1_16, l, BF16, tm=1024, tn=1024, relu2=True, name="ffn_up")
        y = _matmul(u, w_ff2_16, l, BF16, tm=1024, tn=1024, tk=4096, name="ffn_down")
        if l + 1 < depth:
            h, hn = _resid_norm(h, y, post_ffn, l, pre_mix, l + 1)
        else:
            h, _ = _resid_norm(h, y, post_ffn, l)

    return h.reshape(batch, seq, d)
```

```python
import functools

import jax
import jax.numpy as jnp
from jax import lax
from jax.experimental import pallas as pl
from jax.experimental.pallas import tpu as pltpu

F32 = jnp.float32
BF16 = jnp.bfloat16

D_MODEL = 4096
DEPTH = 4
HG_HEADS = 8
HG_DK = 128
HG_DV = 128
FOX_HEADS = 8
FOX_DIM = 128
MLA_HEADS = 16
MLA_Q_LORA = 768
MLA_KV_LORA = 512
MLA_NOPE = 128
MLA_ROPE = 64
MLA_V = 128
ROPE_THETA = 10000.0
MEM_HEADS = 4
MEM_DIM = 128
EPS = 1e-6
MASK_VALUE = -1e30
LB_FLOOR = 1e-30
LOG2E = 1.4426950408889634

LANES = 128
SUBLANES = 8
HG_CHUNK = 64
HG_SUB = 16
HG_UNROLL = 8
ATT_W = 2 * LANES
FLASH_TQ = 2048
FLASH_SUB = 512
FLASH_HP = 2
FLASH_KV_PER_TRIP = 1

P16_HQ, P16_HI, P16_HG = 0, 8, 16
P16_FQKV = 3
P16_COLS = 6 * 1024
IN_TAIL0 = 7 * 1024
TAIL_W = 11 * LANES
TAIL_SHIFT = FOX_HEADS
P32_HF = 11
P32_COLS = 20 * LANES
FF_LANE = 0


def _cparams(sem, vmem_mb=None):
    kw = dict(dimension_semantics=sem)
    if vmem_mb is not None:
        kw["vmem_limit_bytes"] = vmem_mb * 1024 * 1024
    return pltpu.CompilerParams(**kw)


def _mm_kernel(a_ref, b_ref, o_ref, *scratch, nk, relu2, trans_b):
    def dot(a, b):
        dims = (((1,), (1,)), ((), ())) if trans_b else (((1,), (0,)), ((), ()))
        return lax.dot_general(a, b, dims, preferred_element_type=F32)

    def finish(r):
        if relu2:
            r = jnp.square(jnp.maximum(r, 0.0))
        o_ref[...] = r.astype(o_ref.dtype)

    if nk == 1:
        finish(dot(a_ref[...], b_ref[...]))
        return
    (acc_ref,) = scratch
    k = pl.program_id(2)

    @pl.when(k == 0)
    def _():
        acc_ref[...] = jnp.zeros_like(acc_ref)

    acc_ref[...] += dot(a_ref[...], b_ref[...])

    @pl.when(k == nk - 1)
    def _():
        finish(acc_ref[...])


def _matmul(a, w, layer, out_dtype, *, tm, tn, tk=None, relu2=False, trans_b=False, name="mm"):
    m, kdim = a.shape
    n = w.shape[-2] if trans_b else w.shape[-1]
    tk = kdim if tk is None else min(tk, kdim)
    tm, tn = min(tm, m), min(tn, n)
    assert m % tm == 0 and n % tn == 0 and kdim % tk == 0
    nk = kdim // tk
    scratch = [] if nk == 1 else [pltpu.VMEM((tm, tn), F32)]
    return pl.pallas_call(
        functools.partial(_mm_kernel, nk=nk, relu2=relu2, trans_b=trans_b),
        grid=(m // tm, n // tn, nk),
        in_specs=[
            pl.BlockSpec((tm, tk), lambda i, j, k: (i, k)),
            pl.BlockSpec((None, tn, tk), lambda i, j, k: (layer, j, k)) if trans_b else
            pl.BlockSpec((None, tk, tn), lambda i, j, k: (layer, k, j)),
        ],
        out_specs=pl.BlockSpec((tm, tn), lambda i, j, k: (i, j)),
        out_shape=jax.ShapeDtypeStruct((m, n), out_dtype),
        scratch_shapes=scratch,
        compiler_params=_cparams(("parallel", "parallel", "arbitrary"), 56),
        name=name,
    )(a, w)


def _mm_parts_kernel(*refs, widths):
    a_refs, b_ref, o_ref = refs[:len(widths)], refs[len(widths)], refs[len(widths) + 1]
    acc, off = None, 0
    for a_ref, wd in zip(a_refs, widths):
        r = jnp.dot(a_ref[...], b_ref[off:off + wd, :], preferred_element_type=F32)
        acc = r if acc is None else acc + r
        off += wd
    o_ref[...] = acc.astype(o_ref.dtype)


def _matmul_parts(parts, w, layer, out_dtype, *, tm, tn, name):
    m = parts[0].shape[0]
    widths = tuple(p.shape[1] for p in parts)
    kdim, n = sum(widths), w.shape[-1]
    tm, tn = min(tm, m), min(tn, n)
    assert m % tm == 0 and n % tn == 0 and w.shape[-2] == kdim
    return pl.pallas_call(
        functools.partial(_mm_parts_kernel, widths=widths),
        grid=(m // tm, n // tn),
        in_specs=[pl.BlockSpec((tm, wd), lambda i, j: (i, 0)) for wd in widths]
        + [pl.BlockSpec((None, kdim, tn), lambda i, j: (layer, 0, j))],
        out_specs=pl.BlockSpec((tm, tn), lambda i, j: (i, j)),
        out_shape=jax.ShapeDtypeStruct((m, n), out_dtype),
        compiler_params=_cparams(("parallel", "parallel"), 56),
        name=name,
    )(*parts, w)


def _rms(x, g):
    return x * lax.rsqrt(jnp.mean(x * x, axis=-1, keepdims=True) + EPS) * g


def _rms_cast_kernel(x_ref, g_ref, o_ref):
    o_ref[...] = _rms(x_ref[...].astype(F32), g_ref[...]).astype(o_ref.dtype)


def _rms_cast(x, g, layer, *, tm=256, name="rms_cast"):
    m, d = x.shape
    tm = min(tm, m)
    return pl.pallas_call(
        _rms_cast_kernel,
        grid=(m // tm,),
        in_specs=[pl.BlockSpec((tm, d), lambda i: (i, 0)),
                  pl.BlockSpec((None, 1, d), lambda i: (layer, 0, 0))],
        out_specs=pl.BlockSpec((tm, d), lambda i: (i, 0)),
        out_shape=jax.ShapeDtypeStruct((m, d), BF16),
        compiler_params=_cparams(("parallel",)),
        name=name,
    )(x, g)


def _resid_norm_kernel(h_ref, y_ref, gp_ref, *rest, with_next):
    h_new = h_ref[...] + _rms(y_ref[...].astype(F32), gp_ref[...])
    if with_next:
        gn_ref, ho_ref, hn_ref = rest
        hn_ref[...] = _rms(h_new, gn_ref[...]).astype(hn_ref.dtype)
    else:
        (ho_ref,) = rest
    ho_ref[...] = h_new


def _resid_norm(h, y, g_post, layer, g_next=None, layer_next=None, *, tm=256):
    m, d = h.shape
    tm = min(tm, m)
    row = pl.BlockSpec((tm, d), lambda i: (i, 0))
    with_next = g_next is not None
    in_specs = [row, row, pl.BlockSpec((None, 1, d), lambda i: (layer, 0, 0))]
    args = [h, y, g_post]
    out_shape = [jax.ShapeDtypeStruct((m, d), F32)]
    out_specs = [row]
    if with_next:
        in_specs.append(pl.BlockSpec((None, 1, d), lambda i: (layer_next, 0, 0)))
        args.append(g_next)
        out_shape.append(jax.ShapeDtypeStruct((m, d), BF16))
        out_specs.append(row)
    res = pl.pallas_call(
        functools.partial(_resid_norm_kernel, with_next=with_next),
        grid=(m // tm,),
        in_specs=in_specs,
        out_specs=out_specs,
        out_shape=out_shape,
        compiler_params=_cparams(("parallel",)),
        name="resid_norm",
    )(*args)
    return (res[0], res[1]) if with_next else (res[0], None)


def _lower_bound_kernel(x_ref, llb_ref, l1m_ref):
    depth = x_ref.shape[0]
    rows = [x_ref[i:i + 1, :].astype(F32) for i in range(depth)]
    mx = functools.reduce(jnp.maximum, rows)
    ex = [jnp.exp(r - mx) for r in rows]
    tot = functools.reduce(lambda a, b: a + b, ex)
    p = [e / tot for e in ex]
    cum = p[0]
    for i in range(depth):
        if i > 0:
            cum = cum + p[i]
        lb = jnp.clip(cum - p[0], 0.0, 1.0 - 1e-6)
        llb_ref[i:i + 1, :] = jnp.log(jnp.maximum(lb, LB_FLOOR))
        l1m_ref[i:i + 1, :] = jnp.log1p(-lb)


def _lower_bounds(logits):
    shp = jax.ShapeDtypeStruct(logits.shape, F32)
    return pl.pallas_call(_lower_bound_kernel, out_shape=[shp, shp], name="hgrn_lower_bounds")(logits)


def _hgrn_kernel(q_ref, v_ref, gate_ref, z_ref, llb_ref, l1m_ref, g_ref, o_ref,
                 state_ref, b_scr, k_scr, v_scr, *, n_chunks):
    c_len = HG_CHUNK
    n_rows = c_len // SUBLANES

    @pl.when(pl.program_id(2) == 0)
    def _():
        state_ref[...] = jnp.zeros_like(state_ref)

    llb = llb_ref[...]
    l1m = l1m_ref[...]
    gain = g_ref[...]
    r_i = lax.broadcasted_iota(jnp.int32, (c_len, c_len), 0)
    c_i = lax.broadcasted_iota(jnp.int32, (c_len, c_len), 1)
    tri = (c_i <= r_i).astype(F32)
    sub = lax.broadcasted_iota(jnp.int32, (SUBLANES, LANES), 0)
    takes = [(c_i >= jb * HG_SUB) & (c_i < (jb + 1) * HG_SUB) & (r_i >= (jb + 1) * HG_SUB)
             for jb in range(c_len // HG_SUB - 1)]

    def chunk(c, slot, state_t):
        rows = pl.ds(pl.multiple_of(c * c_len, c_len), c_len)
        z = z_ref[rows, :]
        q = q_ref[rows, :].astype(F32)
        v16 = v_ref[rows, :]
        sp = jnp.log1p(jnp.exp(-jnp.abs(z)))
        ls = jnp.minimum(z, 0.0) - sp
        a2 = l1m + ls
        log_f = jnp.maximum(llb, a2) + jnp.log1p(jnp.exp(-jnp.abs(llb - a2)))
        kk = jnp.exp(l1m + (ls - z))
        b = jnp.dot(tri, log_f, preferred_element_type=F32, precision=lax.Precision.HIGHEST) * LOG2E
        b_scr[slot] = b
        k_scr[slot] = kk
        v_scr[slot] = v16.astype(F32)

        q_rows = [q[r * SUBLANES:(r + 1) * SUBLANES, :] for r in range(n_rows)]
        b_rows = [b[r * SUBLANES:(r + 1) * SUBLANES, :] for r in range(n_rows)]
        o_rows = [jnp.zeros((SUBLANES, LANES), F32) for _ in range(n_rows)]
        rows_per_sb = HG_SUB // SUBLANES
        for g in range(n_rows):
            r_hi = (g // rows_per_sb + 1) * rows_per_sb
            for i in range(SUBLANES):
                s = g * SUBLANES + i
                bs = b_scr[slot, s:s + 1, :]
                ks = k_scr[slot, s:s + 1, :]
                vs = v_scr[slot, s:s + 1, :]
                for r in range(g, r_hi):
                    e = jnp.exp2(b_rows[r] - bs)
                    if r == g and i > 0:
                        e = jnp.where(sub >= i, e, 0.0)
                    p = e * (q_rows[r] * ks)
                    o_rows[r] = o_rows[r] + jnp.sum(p, axis=-1, keepdims=True) * vs
        o = jnp.concatenate(o_rows, axis=0)

        n_sb = c_len // HG_SUB
        bend = [b[(jb + 1) * HG_SUB - 1:(jb + 1) * HG_SUB, :] for jb in range(n_sb)]
        bend_rows = jnp.concatenate([jnp.broadcast_to(e_, (HG_SUB, LANES)) for e_ in bend], axis=0)
        khat = (kk * jnp.exp2(bend_rows - b)).astype(BF16)
        a_stack = jnp.concatenate(
            [q * jnp.exp2(jnp.minimum(b - bend[jb], 0.0)) for jb in range(n_sb - 1)], axis=0).astype(BF16)
        r_all = lax.dot_general(a_stack, khat, (((1,), (1,)), ((), ())), preferred_element_type=F32)
        s_off = jnp.zeros((c_len, c_len), F32)
        for jb in range(n_sb - 1):
            s_off = jnp.where(takes[jb], r_all[jb * c_len:(jb + 1) * c_len, :], s_off)
        o = o + jnp.dot(s_off.astype(BF16), v16, preferred_element_type=F32)

        qe = (q * jnp.exp2(b)).astype(BF16)
        o = o + lax.dot_general(qe, state_t.astype(BF16), (((1,), (1,)), ((), ())),
                                preferred_element_type=F32)
        b_end = b[c_len - 1:c_len, :]
        kd = (kk * jnp.exp2(b_end - b)).astype(BF16)
        upd = lax.dot_general(v16, kd, (((0,), (0,)), ((), ())), preferred_element_type=F32)
        state_t = state_t * jnp.exp2(b_end) + upd

        gate = gate_ref[rows, :].astype(F32)
        y = _rms(o, gain) * (gate * jax.nn.sigmoid(gate))
        o_ref[rows, :] = y.astype(o_ref.dtype)
        return state_t

    def trip(cc, state_t):
        for slot in range(HG_UNROLL):
            state_t = chunk(HG_UNROLL * cc + slot, slot, state_t)
        return state_t

    state_ref[...] = lax.fori_loop(0, n_chunks // HG_UNROLL, trip, state_ref[...])


def _hgrn(proj16, proj32, llb, l1m, mix_g, layer, *, batch, seq, blk=512):
    blk = min(blk, seq)
    ns = seq // blk
    assert (blk // HG_CHUNK) % HG_UNROLL == 0
    tok = lambda col0: pl.BlockSpec((blk, LANES), lambda b, h, s: (b * ns + s, col0 + h))
    par = pl.BlockSpec((None, 1, LANES), lambda b, h, s: (layer, 0, h))
    return pl.pallas_call(
        functools.partial(_hgrn_kernel, n_chunks=blk // HG_CHUNK),
        grid=(batch, HG_HEADS, ns),
        in_specs=[tok(P16_HQ), tok(P16_HI), tok(P16_HG), tok(P32_HF), par, par, par],
        out_specs=pl.BlockSpec((blk, LANES), lambda b, h, s: (b * ns + s, h)),
        out_shape=jax.ShapeDtypeStruct((batch * seq, HG_HEADS * HG_DV), BF16),
        scratch_shapes=[pltpu.VMEM((HG_DV, HG_DK), F32)] + [pltpu.VMEM((HG_UNROLL, HG_CHUNK, LANES), F32)] * 3,
        compiler_params=_cparams(("parallel", "parallel", "arbitrary")),
        name="hgrn2_scan",
    )(proj16, proj16, proj16, proj32, llb, l1m, mix_g)


def _fox_prep_kernel(x_ref, bias_ref, fq_ref, fk_ref, fv_ref, qo_ref, ko_ref, vo_ref, carry_ref, *, blk):
    @pl.when(pl.program_id(1) == 0)
    def _():
        carry_ref[...] = jnp.zeros_like(carry_ref)

    x = x_ref[...] + bias_ref[...]
    lf = jnp.minimum(x, 0.0) - jnp.log1p(jnp.exp(-jnp.abs(x)))
    r_i = lax.broadcasted_iota(jnp.int32, (blk, blk), 0)
    c_i = lax.broadcasted_iota(jnp.int32, (blk, blk), 1)
    tri = (c_i <= r_i).astype(F32)
    cum = jnp.dot(tri, lf, preferred_element_type=F32, precision=lax.Precision.HIGHEST) + carry_ref[...]
    carry_ref[...] = cum[blk - 1:blk, :]
    c2 = cum * LOG2E
    lane = lax.broadcasted_iota(jnp.int32, (blk, LANES), 1)
    ones = jnp.ones((blk, LANES), BF16)
    for h in range(FOX_HEADS):
        col = jnp.broadcast_to(c2[:, FF_LANE + h:FF_LANE + h + 1], (blk, LANES))
        hi = col.astype(BF16).astype(F32)
        r1 = col - hi
        mid = r1.astype(BF16).astype(F32)
        lo = r1 - mid
        pieces = jnp.where((lane == 0) | (lane == 3), hi, jnp.where((lane == 1) | (lane == 4), mid, lo))
        q_ext = jnp.where(lane < 3, pieces, jnp.where(lane < 6, 1.0, 0.0))
        k_ext = jnp.where(lane < 3, 1.0, jnp.where(lane < 6, -pieces, 0.0))
        sl = slice(h * LANES, (h + 1) * LANES)
        qo_ref[h, :, :LANES] = fq_ref[:, sl]
        qo_ref[h, :, LANES:] = q_ext.astype(BF16)
        ko_ref[h, :, :LANES] = fk_ref[:, sl]
        ko_ref[h, :, LANES:] = k_ext.astype(BF16)
        vo_ref[h, :, :LANES] = fv_ref[:, sl]
        vo_ref[h, :, LANES:] = ones


def _fox_prep(proj16, proj32, bias_row, layer, *, batch, seq, blk=256):
    blk = min(blk, seq)
    nb = seq // blk
    t = batch * seq
    w = FOX_HEADS * FOX_DIM
    tok = lambda c: pl.BlockSpec((blk, w), lambda b, j: (b * nb + j, P16_FQKV + c))
    out = pl.BlockSpec((FOX_HEADS, blk, ATT_W), lambda b, j: (0, b * nb + j, 0))
    shp = jax.ShapeDtypeStruct((FOX_HEADS, t, ATT_W), BF16)
    return pl.pallas_call(
        functools.partial(_fox_prep_kernel, blk=blk),
        grid=(batch, nb),
        in_specs=[pl.BlockSpec((blk, LANES), lambda b, j: (b * nb + j, 0)),
                  pl.BlockSpec((None, 1, LANES), lambda b, j: (layer, 0, 0)),
                  tok(0), tok(1), tok(2)],
        out_specs=[out, out, out],
        out_shape=[shp, shp, shp],
        scratch_shapes=[pltpu.VMEM((1, LANES), F32)],
        compiler_params=_cparams(("parallel", "arbitrary")),
        name="fox_prep",
    )(proj32, bias_row, proj16, proj16, proj16)


def _flash_kernel(q_ref, k_ref, v_ref, g_ref, o_ref, m_scr, acc_scr, *, tq, sub, hp):
    i = pl.program_id(2)
    n_sub = tq // sub
    m_scr[...] = jnp.full_like(m_scr, MASK_VALUE)
    acc_scr[...] = jnp.zeros_like(acc_scr)
    r_i = lax.broadcasted_iota(jnp.int32, (sub, sub), 0)
    c_i = lax.broadcasted_iota(jnp.int32, (sub, sub), 1)

    def step(hd, r, kv0, diagonal):
        rows = slice(r * sub, (r + 1) * sub)
        keys = pl.ds(kv0, sub)
        s = lax.dot_general(q_ref[hd, rows, :], k_ref[hd, keys, :], (((1,), (1,)), ((), ())),
                            preferred_element_type=F32)
        if diagonal:
            s = jnp.where(c_i <= r_i, s, MASK_VALUE)
        m_prev = m_scr[hd, rows, :]
        m_new = jnp.maximum(m_prev, jnp.max(s, axis=-1, keepdims=True))
        alpha = jnp.exp2(m_prev - m_new)
        p = jnp.exp2(s - jnp.concatenate([m_new] * (sub // LANES), axis=1)).astype(BF16)
        pv = jnp.dot(p, v_ref[hd, keys, :], preferred_element_type=F32)
        acc_scr[hd, rows, :] = jnp.concatenate([alpha, alpha], axis=1) * acc_scr[hd, rows, :] + pv
        m_scr[hd, rows, :] = m_new

    def full_chunks(j, carry):
        kv0 = pl.multiple_of(j * (FLASH_KV_PER_TRIP * sub), FLASH_KV_PER_TRIP * sub)
        for jj in range(FLASH_KV_PER_TRIP):
            for hd in range(hp):
                for r in range(n_sub):
                    step(hd, r, kv0 + jj * sub, False)
        return carry

    assert n_sub % FLASH_KV_PER_TRIP == 0
    lax.fori_loop(0, i * n_sub // FLASH_KV_PER_TRIP, full_chunks, 0)
    base = pl.multiple_of(i * tq, tq)
    for jj in range(n_sub):
        for hd in range(hp):
            for r in range(jj, n_sub):
                step(hd, r, base + jj * sub, jj == r)

    for hd in range(hp):
        acc = acc_scr[hd]
        o = acc[:, :LANES] / acc[:, LANES:]
        cols = slice(hd * LANES, (hd + 1) * LANES)
        o_ref[:, cols] = _rms(o, g_ref[:, cols]).astype(o_ref.dtype)


def _flash(q, k, v, gain, layer, g_off, *, batch, seq, name):
    heads, t, _ = q.shape
    tq, sub, hp = min(FLASH_TQ, seq), min(FLASH_SUB, seq), FLASH_HP
    nq = seq // tq
    assert heads % hp == 0 and g_off % hp == 0
    kv = pl.BlockSpec((hp, seq, ATT_W), lambda b, h, i: (h, b, 0))
    return pl.pallas_call(
        functools.partial(_flash_kernel, tq=tq, sub=sub, hp=hp),
        grid=(batch, heads // hp, nq),
        in_specs=[pl.BlockSpec((hp, tq, ATT_W), lambda b, h, i: (h, b * nq + i, 0)), kv, kv,
                  pl.BlockSpec((None, 1, hp * LANES), lambda b, h, i: (layer, 0, g_off // hp + h))],
        out_specs=pl.BlockSpec((tq, hp * LANES), lambda b, h, i: (b * nq + i, h)),
        out_shape=jax.ShapeDtypeStruct((t, heads * LANES), BF16),
        scratch_shapes=[pltpu.VMEM((hp, tq, LANES), F32), pltpu.VMEM((hp, tq, ATT_W), F32)],
        compiler_params=_cparams(("parallel", "parallel", "arbitrary"), 56),
        name=name,
    )(q, k, v, gain)


def _rope128(x, tab):
    c, s1, s2 = tab[:, :LANES], tab[:, LANES:2 * LANES], tab[:, 2 * LANES:]
    half = MLA_ROPE // 2
    return x * c + pltpu.roll(x, LANES - half, axis=1) * s1 + pltpu.roll(x, half, axis=1) * s2


def _mla_prep_kernel(tail_ref, ang_ref, gq_ref, gkv_ref, cqn_ref, ckvn_ref, krot_ref, tab_ref):
    tail = pltpu.roll(tail_ref[...], TAIL_W - TAIL_SHIFT, axis=1)
    cq = tail[:, :MLA_Q_LORA]
    ckv = tail[:, MLA_Q_LORA:MLA_Q_LORA + MLA_KV_LORA]
    misc = tail[:, MLA_Q_LORA + MLA_KV_LORA:]
    cqn_ref[...] = _rms(cq, gq_ref[...]).astype(cqn_ref.dtype)
    ckvn_ref[...] = _rms(ckv, gkv_ref[...]).astype(ckvn_ref.dtype)
    ang = ang_ref[...]
    cos, sin = jnp.cos(ang), jnp.sin(ang)
    lane = lax.broadcasted_iota(jnp.int32, ang.shape, 1)
    half = MLA_ROPE // 2
    tab = jnp.concatenate([
        jnp.where(lane < MLA_ROPE, cos, 0.0),
        jnp.where(lane < half, -sin, 0.0),
        jnp.where((lane >= half) & (lane < MLA_ROPE), sin, 0.0)], axis=1)
    tab_ref[...] = tab
    krot_ref[...] = _rope128(misc, tab).astype(krot_ref.dtype)


def _mla_prep(proj32, ang, gq, gkv, layer, *, tm=512):
    t = proj32.shape[0]
    tm = min(tm, t)
    return pl.pallas_call(
        _mla_prep_kernel,
        grid=(t // tm,),
        in_specs=[pl.BlockSpec((tm, TAIL_W), lambda i: (i, 0)),
                  pl.BlockSpec((tm, LANES), lambda i: (i, 0)),
                  pl.BlockSpec((None, 1, MLA_Q_LORA), lambda i: (layer, 0, 0)),
                  pl.BlockSpec((None, 1, MLA_KV_LORA), lambda i: (layer, 0, 0))],
        out_specs=[pl.BlockSpec((tm, MLA_Q_LORA), lambda i: (i, 0)),
                   pl.BlockSpec((tm, MLA_KV_LORA), lambda i: (i, 0)),
                   pl.BlockSpec((tm, LANES), lambda i: (i, 0)),
                   pl.BlockSpec((tm, 3 * LANES), lambda i: (i, 0))],
        out_shape=[jax.ShapeDtypeStruct((t, MLA_Q_LORA), BF16),
                   jax.ShapeDtypeStruct((t, MLA_KV_LORA), BF16),
                   jax.ShapeDtypeStruct((t, LANES), BF16),
                   jax.ShapeDtypeStruct((t, 3 * LANES), F32)],
        compiler_params=_cparams(("parallel",)),
        name="mla_prep",
    )(proj32, ang, gq, gkv)


def _mla_proj_kernel(cqn_ref, ckvn_ref, krot_ref, tab_ref, wq_ref, wkv_ref, q_ref, k_ref, v_ref):
    q = jnp.dot(cqn_ref[...], wq_ref[...], preferred_element_type=F32)
    q_ref[:, :LANES] = q[:, :LANES].astype(q_ref.dtype)
    q_ref[:, LANES:] = _rope128(q[:, LANES:], tab_ref[...]).astype(q_ref.dtype)
    kv = jnp.dot(ckvn_ref[...], wkv_ref[...], preferred_element_type=F32)
    k_ref[:, :LANES] = kv[:, :LANES].astype(k_ref.dtype)
    k_ref[:, LANES:] = krot_ref[...]
    v_ref[:, :LANES] = kv[:, LANES:].astype(v_ref.dtype)
    v_ref[:, LANES:] = jnp.ones((v_ref.shape[0], LANES), v_ref.dtype)


def _mla_proj(cqn, ckvn, krot, tab, wq, wkv, layer, *, tm=2048):
    t = cqn.shape[0]
    tm = min(tm, t)
    tok = lambda w: pl.BlockSpec((tm, w), lambda i, h: (i, 0))
    out = pl.BlockSpec((None, tm, ATT_W), lambda i, h: (h, i, 0))
    shp = jax.ShapeDtypeStruct((MLA_HEADS, t, ATT_W), BF16)
    return pl.pallas_call(
        _mla_proj_kernel,
        grid=(t // tm, MLA_HEADS),
        in_specs=[tok(MLA_Q_LORA), tok(MLA_KV_LORA), tok(LANES), tok(3 * LANES),
                  pl.BlockSpec((None, None, MLA_Q_LORA, ATT_W), lambda i, h: (layer, h, 0, 0)),
                  pl.BlockSpec((None, None, MLA_KV_LORA, 2 * LANES), lambda i, h: (layer, h, 0, 0))],
        out_specs=[out, out, out],
        out_shape=[shp, shp, shp],
        compiler_params=_cparams(("parallel", "parallel")),
        name="mla_proj",
    )(cqn, ckvn, krot, tab, wq, wkv)


def _mem_block_kernel(hn_ref, h_ref, wq_ref, k_ref, v_ref, wo_ref, gp_ref, gn_ref, ho_ref, hno_ref):
    scale = MEM_DIM ** -0.5
    q = jnp.dot(hn_ref[...], wq_ref[...], preferred_element_type=F32).astype(BF16)
    outs = []
    for h in range(MEM_HEADS):
        sl = slice(h * MEM_DIM, (h + 1) * MEM_DIM)
        s = lax.dot_general(q[:, sl], k_ref[:, sl], (((1,), (1,)), ((), ())),
                            preferred_element_type=F32) * scale
        p = jnp.exp(s - jnp.max(s, axis=-1, keepdims=True))
        p = p / jnp.sum(p, axis=-1, keepdims=True)
        outs.append(jnp.dot(p.astype(BF16), v_ref[:, sl], preferred_element_type=F32))
    o = jnp.concatenate(outs, axis=1).astype(BF16)
    y = jnp.dot(o, wo_ref[...], preferred_element_type=F32)
    h_new = h_ref[...] + _rms(y, gp_ref[...])
    ho_ref[...] = h_new
    hno_ref[...] = _rms(h_new, gn_ref[...]).astype(hno_ref.dtype)


def _mem_block(hn, h, w_mq, km, vm, w_mo, g_post, g_next, layer, *, seq, mem_tokens, tm=256):
    t, d = h.shape
    tm = min(tm, seq)
    nt = seq // tm
    w = MEM_HEADS * MEM_DIM
    row = pl.BlockSpec((tm, d), lambda i: (i, 0))
    kv = pl.BlockSpec((mem_tokens, w), lambda i: (i // nt, 0))
    gain = pl.BlockSpec((None, 1, d), lambda i: (layer, 0, 0))
    return pl.pallas_call(
        _mem_block_kernel,
        grid=(t // tm,),
        in_specs=[row, row, pl.BlockSpec((None, d, w), lambda i: (layer, 0, 0)), kv, kv,
                  pl.BlockSpec((None, w, d), lambda i: (layer, 0, 0)), gain, gain],
        out_specs=[row, row],
        out_shape=[jax.ShapeDtypeStruct((t, d), F32), jax.ShapeDtypeStruct((t, d), BF16)],
        compiler_params=_cparams(("parallel",), 56),
        name="mem_block",
    )(hn, h, w_mq, km, vm, w_mo, g_post, g_next)


def _split_w_in_kernel(w_ref, w16_ref, w32_ref):
    cast = lambda r0, r1: w_ref[r0:r1, :].astype(BF16)
    w16_ref[:1024, :] = cast(0, 1024)
    w16_ref[1024:3072, :] = cast(2048, 4096)
    w16_ref[3072:4096, :] = (w_ref[4096:5120, :] * (FOX_DIM ** -0.5 * LOG2E)).astype(BF16)
    w16_ref[4096:, :] = cast(5120, IN_TAIL0)
    n_tail, tc = w_ref.shape[0] - IN_TAIL0, w_ref.shape[1]
    tail = jnp.concatenate([w_ref[IN_TAIL0:, :], jnp.zeros((TAIL_W - n_tail, tc), F32)], axis=0)
    w32_ref[:TAIL_W, :] = tail.astype(BF16)
    w32_ref[TAIL_W:TAIL_W + 1024, :] = cast(1024, 2048)
    w32_ref[TAIL_W + 1024:, :] = jnp.zeros((P32_COLS - TAIL_W - 1024, tc), BF16)


def _split_w_in(w_in, *, tc=256):
    w_t = jnp.swapaxes(w_in, 1, 2)
    depth, cols, d = w_t.shape
    tc = min(tc, d)
    blk = lambda rows: pl.BlockSpec((None, rows, tc), lambda l, c: (l, 0, c))
    return pl.pallas_call(
        _split_w_in_kernel,
        grid=(depth, d // tc),
        in_specs=[blk(cols)],
        out_specs=[blk(P16_COLS), blk(P32_COLS)],
        out_shape=[jax.ShapeDtypeStruct((depth, P16_COLS, d), BF16),
                   jax.ShapeDtypeStruct((depth, P32_COLS, d), BF16)],
        compiler_params=_cparams(("parallel", "parallel"), 56),
        name="w_in_split",
    )(w_t)


def _split_mla_weights(w_uq, w_ukv):
    depth = w_uq.shape[0]
    wq = w_uq.reshape(depth, MLA_Q_LORA, MLA_HEADS, MLA_NOPE + MLA_ROPE).transpose(0, 2, 1, 3)
    wq = wq * ((MLA_NOPE + MLA_ROPE) ** -0.5 * LOG2E)
    wq = jnp.pad(wq, ((0, 0), (0, 0), (0, 0), (0, ATT_W - MLA_NOPE - MLA_ROPE)))
    wkv = w_ukv.reshape(depth, MLA_KV_LORA, MLA_HEADS, MLA_NOPE + MLA_V).transpose(0, 2, 1, 3)
    return wq.astype(BF16), wkv.astype(BF16)


def kernel(x, mem, positions, w_in, hg_lb_logits, fox_f_bias, mla_q_norm_g, mla_kv_norm_g, w_uq, w_ukv,
           mix_out_g, w_o, mem_norm_g, w_mq, w_mk, w_mv, w_mo, w_ff1, w_ff2, pre_mix_g, post_mix_g,
           pre_mem_g, post_mem_g, pre_ffn_g, post_ffn_g):
    batch, seq, d = x.shape
    depth = w_in.shape[0]
    t = batch * seq
    mem_tokens = mem.shape[1]

    w16, w32 = _split_w_in(w_in)
    wq, wkv = _split_mla_weights(w_uq, w_ukv)
    w_o16, w_mq16, w_mk16, w_mv16, w_mo16 = (w.astype(BF16) for w in (w_o, w_mq, w_mk, w_mv, w_mo))
    w_ff1_16, w_ff2_16 = w_ff1.astype(BF16), w_ff2.astype(BF16)
    row = lambda g: g.astype(F32).reshape(depth, 1, g.shape[-1])
    pre_mix, post_mix, pre_mem, post_mem, pre_ffn, post_ffn = map(
        row, (pre_mix_g, post_mix_g, pre_mem_g, post_mem_g, pre_ffn_g, post_ffn_g))
    mix_g, mem_g, gq, gkv = map(row, (mix_out_g, mem_norm_g, mla_q_norm_g, mla_kv_norm_g))
    fox_bias = jnp.pad(fox_f_bias.astype(F32), ((0, 0), (FF_LANE, LANES - FF_LANE - FOX_HEADS)))
    fox_bias = fox_bias.reshape(depth, 1, LANES)
    llb, l1m = _lower_bounds(hg_lb_logits)
    llb, l1m = row(llb), row(l1m)

    half = MLA_ROPE // 2
    inv = ROPE_THETA ** (-jnp.arange(half, dtype=F32) / half)
    ang = positions.reshape(t, 1).astype(F32) * inv[None, :]
    ang = jnp.concatenate([ang, ang, jnp.zeros((t, LANES - MLA_ROPE), F32)], axis=1)

    h = x.reshape(t, d)
    mem2 = mem.reshape(batch * mem_tokens, d)
    hn = _rms_cast(h, pre_mix, 0, name="rms_first")

    for l in range(depth):
        proj16 = _matmul(hn, w16, l, BF16, tm=1024, tn=1024, trans_b=True, name="in_proj16")
        proj32 = _matmul(hn, w32, l, F32, tm=1024, tn=P32_COLS // 2, trans_b=True, name="in_proj32")

        mix_a = _hgrn(proj16, proj32, llb, l1m, mix_g, l, batch=batch, seq=seq)

        fq, fk, fv = _fox_prep(proj16, proj32, fox_bias, l, batch=batch, seq=seq)
        mix_b = _flash(fq, fk, fv, mix_g, l, HG_HEADS, batch=batch, seq=seq, name="fox_attn")

        cqn, ckvn, krot, tab = _mla_prep(proj32, ang, gq, gkv, l)
        q_c, k_c, v_c = _mla_proj(cqn, ckvn, krot, tab, wq, wkv, l)
        mix_c = _flash(q_c, k_c, v_c, mix_g, l, HG_HEADS + FOX_HEADS, batch=batch, seq=seq, name="mla_attn")

        y = _matmul_parts([mix_a, mix_b, mix_c], w_o16, l, BF16, tm=1024, tn=1024, name="mix_out")
        h, hn = _resid_norm(h, y, post_mix, l, pre_mem, l)

        mem_n = _rms_cast(mem2, mem_g, l, name="rms_mem")
        km = _matmul(mem_n, w_mk16, l, BF16, tm=512, tn=512, name="mem_k")
        vm = _matmul(mem_n, w_mv16, l, BF16, tm=512, tn=512, name="mem_v")
        h, hn = _mem_block(hn, h, w_mq16, km, vm, w_mo16, post_mem, pre_ffn, l, seq=seq, mem_tokens=mem_tokens)

        u = _matmul(hn, w_ff1_16, l, BF16, tm=1024, tn=1024, relu2=True, name="ffn_up")
        y = _matmul(u, w_ff2_16, l, BF16, tm=1024, tn=1024, tk=4096, name="ffn_down")
        if l + 1 < depth:
            h, hn = _resid_norm(h, y, post_ffn, l, pre_mix, l + 1)
        else:
            h, _ = _resid_norm(h, y, post_ffn, l)

    return h.reshape(batch, seq, d)
```

```python
import functools

import jax
import jax.numpy as jnp
from jax import lax
from jax.experimental import pallas as pl
from jax.experimental.pallas import tpu as pltpu

F32 = jnp.float32
BF16 = jnp.bfloat16

D_MODEL = 4096
DEPTH = 4
HG_HEADS = 8
HG_DK = 128
HG_DV = 128
FOX_HEADS = 8
FOX_DIM = 128
MLA_HEADS = 16
MLA_Q_LORA = 768
MLA_KV_LORA = 512
MLA_NOPE = 128
MLA_ROPE = 64
MLA_V = 128
ROPE_THETA = 10000.0
MEM_HEADS = 4
MEM_DIM = 128
EPS = 1e-6
MASK_VALUE = -1e30
LB_FLOOR = 1e-30
LOG2E = 1.4426950408889634

LANES = 128
SUBLANES = 8
HG_CHUNK = 64
HG_SUB = 16
HG_UNROLL = 8
ATT_W = 2 * LANES
FLASH_TQ = 2048
FLASH_SUB = 512
FLASH_HP = 2
FLASH_WIDE = 4

P16_HQ, P16_HI, P16_HG = 0, 8, 16
P16_FQKV = 3
P16_COLS = 6 * 1024
IN_TAIL0 = 7 * 1024
TAIL_W = 11 * LANES
TAIL_SHIFT = FOX_HEADS
P32_HF = 11
P32_COLS = 20 * LANES
FF_LANE = 0


def _cparams(sem, vmem_mb=None):
    kw = dict(dimension_semantics=sem)
    if vmem_mb is not None:
        kw["vmem_limit_bytes"] = vmem_mb * 1024 * 1024
    return pltpu.CompilerParams(**kw)


def _mm_kernel(a_ref, b_ref, o_ref, *scratch, nk, relu2, trans_b):
    def dot(a, b):
        dims = (((1,), (1,)), ((), ())) if trans_b else (((1,), (0,)), ((), ()))
        return lax.dot_general(a, b, dims, preferred_element_type=F32)

    def finish(r):
        if relu2:
            r = jnp.square(jnp.maximum(r, 0.0))
        o_ref[...] = r.astype(o_ref.dtype)

    if nk == 1:
        finish(dot(a_ref[...], b_ref[...]))
        return
    (acc_ref,) = scratch
    k = pl.program_id(2)

    @pl.when(k == 0)
    def _():
        acc_ref[...] = jnp.zeros_like(acc_ref)

    acc_ref[...] += dot(a_ref[...], b_ref[...])

    @pl.when(k == nk - 1)
    def _():
        finish(acc_ref[...])


def _matmul(a, w, layer, out_dtype, *, tm, tn, tk=None, relu2=False, trans_b=False, name="mm"):
    m, kdim = a.shape
    n = w.shape[-2] if trans_b else w.shape[-1]
    tk = kdim if tk is None else min(tk, kdim)
    tm, tn = min(tm, m), min(tn, n)
    assert m % tm == 0 and n % tn == 0 and kdim % tk == 0
    nk = kdim // tk
    scratch = [] if nk == 1 else [pltpu.VMEM((tm, tn), F32)]
    return pl.pallas_call(
        functools.partial(_mm_kernel, nk=nk, relu2=relu2, trans_b=trans_b),
        grid=(m // tm, n // tn, nk),
        in_specs=[
            pl.BlockSpec((tm, tk), lambda i, j, k: (i, k)),
            pl.BlockSpec((None, tn, tk), lambda i, j, k: (layer, j, k)) if trans_b else
            pl.BlockSpec((None, tk, tn), lambda i, j, k: (layer, k, j)),
        ],
        out_specs=pl.BlockSpec((tm, tn), lambda i, j, k: (i, j)),
        out_shape=jax.ShapeDtypeStruct((m, n), out_dtype),
        scratch_shapes=scratch,
        compiler_params=_cparams(("parallel", "parallel", "arbitrary"), 56),
        name=name,
    )(a, w)


def _mm_parts_kernel(*refs, widths):
    a_refs, b_ref, o_ref = refs[:len(widths)], refs[len(widths)], refs[len(widths) + 1]
    acc, off = None, 0
    for a_ref, wd in zip(a_refs, widths):
        r = jnp.dot(a_ref[...], b_ref[off:off + wd, :], preferred_element_type=F32)
        acc = r if acc is None else acc + r
        off += wd
    o_ref[...] = acc.astype(o_ref.dtype)


def _matmul_parts(parts, w, layer, out_dtype, *, tm, tn, name):
    m = parts[0].shape[0]
    widths = tuple(p.shape[1] for p in parts)
    kdim, n = sum(widths), w.shape[-1]
    tm, tn = min(tm, m), min(tn, n)
    assert m % tm == 0 and n % tn == 0 and w.shape[-2] == kdim
    return pl.pallas_call(
        functools.partial(_mm_parts_kernel, widths=widths),
        grid=(m // tm, n // tn),
        in_specs=[pl.BlockSpec((tm, wd), lambda i, j: (i, 0)) for wd in widths]
        + [pl.BlockSpec((None, kdim, tn), lambda i, j: (layer, 0, j))],
        out_specs=pl.BlockSpec((tm, tn), lambda i, j: (i, j)),
        out_shape=jax.ShapeDtypeStruct((m, n), out_dtype),
        compiler_params=_cparams(("parallel", "parallel"), 56),
        name=name,
    )(*parts, w)


def _rms(x, g):
    return x * lax.rsqrt(jnp.mean(x * x, axis=-1, keepdims=True) + EPS) * g


def _rms_cast_kernel(x_ref, g_ref, o_ref):
    o_ref[...] = _rms(x_ref[...].astype(F32), g_ref[...]).astype(o_ref.dtype)


def _rms_cast(x, g, layer, *, tm=256, name="rms_cast"):
    m, d = x.shape
    tm = min(tm, m)
    return pl.pallas_call(
        _rms_cast_kernel,
        grid=(m // tm,),
        in_specs=[pl.BlockSpec((tm, d), lambda i: (i, 0)),
                  pl.BlockSpec((None, 1, d), lambda i: (layer, 0, 0))],
        out_specs=pl.BlockSpec((tm, d), lambda i: (i, 0)),
        out_shape=jax.ShapeDtypeStruct((m, d), BF16),
        compiler_params=_cparams(("parallel",)),
        name=name,
    )(x, g)


def _resid_norm_kernel(h_ref, y_ref, gp_ref, *rest, with_next):
    h_new = h_ref[...] + _rms(y_ref[...].astype(F32), gp_ref[...])
    if with_next:
        gn_ref, ho_ref, hn_ref = rest
        hn_ref[...] = _rms(h_new, gn_ref[...]).astype(hn_ref.dtype)
    else:
        (ho_ref,) = rest
    ho_ref[...] = h_new


def _resid_norm(h, y, g_post, layer, g_next=None, layer_next=None, *, tm=256):
    m, d = h.shape
    tm = min(tm, m)
    row = pl.BlockSpec((tm, d), lambda i: (i, 0))
    with_next = g_next is not None
    in_specs = [row, row, pl.BlockSpec((None, 1, d), lambda i: (layer, 0, 0))]
    args = [h, y, g_post]
    out_shape = [jax.ShapeDtypeStruct((m, d), F32)]
    out_specs = [row]
    if with_next:
        in_specs.append(pl.BlockSpec((None, 1, d), lambda i: (layer_next, 0, 0)))
        args.append(g_next)
        out_shape.append(jax.ShapeDtypeStruct((m, d), BF16))
        out_specs.append(row)
    res = pl.pallas_call(
        functools.partial(_resid_norm_kernel, with_next=with_next),
        grid=(m // tm,),
        in_specs=in_specs,
        out_specs=out_specs,
        out_shape=out_shape,
        compiler_params=_cparams(("parallel",)),
        name="resid_norm",
    )(*args)
    return (res[0], res[1]) if with_next else (res[0], None)


def _lower_bound_kernel(x_ref, llb_ref, l1m_ref):
    depth = x_ref.shape[0]
    rows = [x_ref[i:i + 1, :].astype(F32) for i in range(depth)]
    mx = functools.reduce(jnp.maximum, rows)
    ex = [jnp.exp(r - mx) for r in rows]
    tot = functools.reduce(lambda a, b: a + b, ex)
    p = [e / tot for e in ex]
    cum = p[0]
    for i in range(depth):
        if i > 0:
            cum = cum + p[i]
        lb = jnp.clip(cum - p[0], 0.0, 1.0 - 1e-6)
        llb_ref[i:i + 1, :] = jnp.log(jnp.maximum(lb, LB_FLOOR))
        l1m_ref[i:i + 1, :] = jnp.log1p(-lb)


def _lower_bounds(logits):
    shp = jax.ShapeDtypeStruct(logits.shape, F32)
    return pl.pallas_call(_lower_bound_kernel, out_shape=[shp, shp], name="hgrn_lower_bounds")(logits)


def _hgrn_kernel(q_ref, v_ref, gate_ref, z_ref, llb_ref, l1m_ref, g_ref, o_ref,
                 state_ref, b_scr, k_scr, v_scr, *, n_chunks):
    c_len = HG_CHUNK
    n_rows = c_len // SUBLANES

    @pl.when(pl.program_id(2) == 0)
    def _():
        state_ref[...] = jnp.zeros_like(state_ref)

    lbf = jnp.exp(llb_ref[...])
    oml = jnp.exp(l1m_ref[...])
    gain = g_ref[...]
    r_i = lax.broadcasted_iota(jnp.int32, (c_len, c_len), 0)
    c_i = lax.broadcasted_iota(jnp.int32, (c_len, c_len), 1)
    tri = (c_i <= r_i).astype(F32)
    sub = lax.broadcasted_iota(jnp.int32, (SUBLANES, LANES), 0)
    takes = [(c_i >= jb * HG_SUB) & (c_i < (jb + 1) * HG_SUB) & (r_i >= (jb + 1) * HG_SUB)
             for jb in range(c_len // HG_SUB - 1)]

    def chunk(c, slot, state_t):
        rows = pl.ds(pl.multiple_of(c * c_len, c_len), c_len)
        z = z_ref[rows, :]
        q = q_ref[rows, :].astype(F32)
        v16 = v_ref[rows, :]
        e = jnp.exp(-jnp.abs(z))
        r = 1.0 / (1.0 + e)
        er = e * r
        pos = z >= 0.0
        log_f = jnp.log(lbf + oml * jnp.where(pos, r, er))
        kk = oml * jnp.where(pos, er, r)
        b = jnp.dot(tri, log_f, preferred_element_type=F32, precision=lax.Precision.HIGHEST) * LOG2E
        b_scr[slot] = b
        k_scr[slot] = kk
        v_scr[slot] = v16.astype(F32)

        q_rows = [q[r * SUBLANES:(r + 1) * SUBLANES, :] for r in range(n_rows)]
        b_rows = [b[r * SUBLANES:(r + 1) * SUBLANES, :] for r in range(n_rows)]
        o_rows = [jnp.zeros((SUBLANES, LANES), F32) for _ in range(n_rows)]
        rows_per_sb = HG_SUB // SUBLANES
        for g in range(n_rows):
            r_hi = (g // rows_per_sb + 1) * rows_per_sb
            for i in range(SUBLANES):
                s = g * SUBLANES + i
                bs = b_scr[slot, s:s + 1, :]
                ks = k_scr[slot, s:s + 1, :]
                vs = v_scr[slot, s:s + 1, :]
                for r in range(g, r_hi):
                    e = jnp.exp2(b_rows[r] - bs)
                    if r == g and i > 0:
                        e = jnp.where(sub >= i, e, 0.0)
                    p = e * (q_rows[r] * ks)
                    o_rows[r] = o_rows[r] + jnp.sum(p, axis=-1, keepdims=True) * vs
        o = jnp.concatenate(o_rows, axis=0)

        n_sb = c_len // HG_SUB
        bend = [b[(jb + 1) * HG_SUB - 1:(jb + 1) * HG_SUB, :] for jb in range(n_sb)]
        bend_rows = jnp.concatenate([jnp.broadcast_to(e_, (HG_SUB, LANES)) for e_ in bend], axis=0)
        khat = (kk * jnp.exp2(bend_rows - b)).astype(BF16)
        a_stack = jnp.concatenate(
            [q * jnp.exp2(jnp.minimum(b - bend[jb], 0.0)) for jb in range(n_sb - 1)], axis=0).astype(BF16)
        r_all = lax.dot_general(a_stack, khat, (((1,), (1,)), ((), ())), preferred_element_type=F32)
        s_off = jnp.zeros((c_len, c_len), F32)
        for jb in range(n_sb - 1):
            s_off = jnp.where(takes[jb], r_all[jb * c_len:(jb + 1) * c_len, :], s_off)
        o = o + jnp.dot(s_off.astype(BF16), v16, preferred_element_type=F32)

        qe = (q * jnp.exp2(b)).astype(BF16)
        o = o + lax.dot_general(qe, state_t.astype(BF16), (((1,), (1,)), ((), ())),
                                preferred_element_type=F32)
        b_end = b[c_len - 1:c_len, :]
        kd = (kk * jnp.exp2(b_end - b)).astype(BF16)
        upd = lax.dot_general(v16, kd, (((0,), (0,)), ((), ())), preferred_element_type=F32)
        state_t = state_t * jnp.exp2(b_end) + upd

        gate = gate_ref[rows, :].astype(F32)
        y = _rms(o, gain) * (gate * jax.nn.sigmoid(gate))
        o_ref[rows, :] = y.astype(o_ref.dtype)
        return state_t

    def trip(cc, state_t):
        for slot in range(HG_UNROLL):
            state_t = chunk(HG_UNROLL * cc + slot, slot, state_t)
        return state_t

    state_ref[...] = lax.fori_loop(0, n_chunks // HG_UNROLL, trip, state_ref[...])


def _hgrn(proj16, proj32, llb, l1m, mix_g, layer, *, batch, seq, blk=512):
    blk = min(blk, seq)
    ns = seq // blk
    assert (blk // HG_CHUNK) % HG_UNROLL == 0
    tok = lambda col0: pl.BlockSpec((blk, LANES), lambda b, h, s: (b * ns + s, col0 + h))
    par = pl.BlockSpec((None, 1, LANES), lambda b, h, s: (layer, 0, h))
    return pl.pallas_call(
        functools.partial(_hgrn_kernel, n_chunks=blk // HG_CHUNK),
        grid=(batch, HG_HEADS, ns),
        in_specs=[tok(P16_HQ), tok(P16_HI), tok(P16_HG), tok(P32_HF), par, par, par],
        out_specs=pl.BlockSpec((blk, LANES), lambda b, h, s: (b * ns + s, h)),
        out_shape=jax.ShapeDtypeStruct((batch * seq, HG_HEADS * HG_DV), BF16),
        scratch_shapes=[pltpu.VMEM((HG_DV, HG_DK), F32)] + [pltpu.VMEM((HG_UNROLL, HG_CHUNK, LANES), F32)] * 3,
        compiler_params=_cparams(("parallel", "parallel", "arbitrary")),
        name="hgrn2_scan",
    )(proj16, proj16, proj16, proj32, llb, l1m, mix_g)


def _fox_prep_kernel(x_ref, bias_ref, fq_ref, fk_ref, fv_ref, qo_ref, ko_ref, vo_ref, carry_ref, *, blk):
    @pl.when(pl.program_id(1) == 0)
    def _():
        carry_ref[...] = jnp.zeros_like(carry_ref)

    x = x_ref[...] + bias_ref[...]
    lf = jnp.minimum(x, 0.0) - jnp.log1p(jnp.exp(-jnp.abs(x)))
    r_i = lax.broadcasted_iota(jnp.int32, (blk, blk), 0)
    c_i = lax.broadcasted_iota(jnp.int32, (blk, blk), 1)
    tri = (c_i <= r_i).astype(F32)
    cum = jnp.dot(tri, lf, preferred_element_type=F32, precision=lax.Precision.HIGHEST) + carry_ref[...]
    carry_ref[...] = cum[blk - 1:blk, :]
    c2 = cum * LOG2E
    lane = lax.broadcasted_iota(jnp.int32, (blk, LANES), 1)
    ones = jnp.ones((blk, LANES), BF16)
    for h in range(FOX_HEADS):
        col = jnp.broadcast_to(c2[:, FF_LANE + h:FF_LANE + h + 1], (blk, LANES))
        hi = col.astype(BF16).astype(F32)
        r1 = col - hi
        mid = r1.astype(BF16).astype(F32)
        lo = r1 - mid
        pieces = jnp.where((lane == 0) | (lane == 3), hi, jnp.where((lane == 1) | (lane == 4), mid, lo))
        q_ext = jnp.where(lane < 3, pieces, jnp.where(lane < 6, 1.0, 0.0))
        k_ext = jnp.where(lane < 3, 1.0, jnp.where(lane < 6, -pieces, 0.0))
        sl = slice(h * LANES, (h + 1) * LANES)
        qo_ref[h, :, :LANES] = fq_ref[:, sl]
        qo_ref[h, :, LANES:] = q_ext.astype(BF16)
        ko_ref[h, :, :LANES] = fk_ref[:, sl]
        ko_ref[h, :, LANES:] = k_ext.astype(BF16)
        vo_ref[h, :, :LANES] = fv_ref[:, sl]
        vo_ref[h, :, LANES:] = ones


def _fox_prep(proj16, proj32, bias_row, layer, *, batch, seq, blk=256):
    blk = min(blk, seq)
    nb = seq // blk
    t = batch * seq
    w = FOX_HEADS * FOX_DIM
    tok = lambda c: pl.BlockSpec((blk, w), lambda b, j: (b * nb + j, P16_FQKV + c))
    out = pl.BlockSpec((FOX_HEADS, blk, ATT_W), lambda b, j: (0, b * nb + j, 0))
    shp = jax.ShapeDtypeStruct((FOX_HEADS, t, ATT_W), BF16)
    return pl.pallas_call(
        functools.partial(_fox_prep_kernel, blk=blk),
        grid=(batch, nb),
        in_specs=[pl.BlockSpec((blk, LANES), lambda b, j: (b * nb + j, 0)),
                  pl.BlockSpec((None, 1, LANES), lambda b, j: (layer, 0, 0)),
                  tok(0), tok(1), tok(2)],
        out_specs=[out, out, out],
        out_shape=[shp, shp, shp],
        scratch_shapes=[pltpu.VMEM((1, LANES), F32)],
        compiler_params=_cparams(("parallel", "arbitrary")),
        name="fox_prep",
    )(proj32, bias_row, proj16, proj16, proj16)


def _flash_kernel(q_ref, k_ref, v_ref, g_ref, o_ref, m_scr, acc_scr, *, tq, sub, hp):
    i = pl.program_id(2)
    n_sub = tq // sub
    m_scr[...] = jnp.full_like(m_scr, MASK_VALUE)
    acc_scr[...] = jnp.zeros_like(acc_scr)
    r_i = lax.broadcasted_iota(jnp.int32, (sub, sub), 0)
    c_i = lax.broadcasted_iota(jnp.int32, (sub, sub), 1)

    def step(hd, r, kv0, n_keys, diagonal):
        rows = slice(r * sub, (r + 1) * sub)
        keys = pl.ds(kv0, n_keys)
        s = lax.dot_general(q_ref[hd, rows, :], k_ref[hd, keys, :], (((1,), (1,)), ((), ())),
                            preferred_element_type=F32)
        if diagonal:
            s = jnp.where(c_i <= r_i, s, MASK_VALUE)
        m_prev = m_scr[hd, rows, :]
        m_new = jnp.maximum(m_prev, jnp.max(s, axis=-1, keepdims=True))
        alpha = jnp.exp2(m_prev - m_new)
        p = jnp.exp2(s - jnp.concatenate([m_new] * (n_keys // LANES), axis=1)).astype(BF16)
        pv = jnp.dot(p, v_ref[hd, keys, :], preferred_element_type=F32)
        acc_scr[hd, rows, :] = jnp.concatenate([alpha, alpha], axis=1) * acc_scr[hd, rows, :] + pv
        m_scr[hd, rows, :] = m_new

    wide = FLASH_WIDE * sub

    def full_chunks(j, carry):
        kv0 = pl.multiple_of(j * wide, wide)
        for hd in range(hp):
            for r in range(n_sub):
                step(hd, r, kv0, wide, False)
        return carry

    assert n_sub % FLASH_WIDE == 0
    lax.fori_loop(0, i * n_sub // FLASH_WIDE, full_chunks, 0)
    base = pl.multiple_of(i * tq, tq)
    for jj in range(n_sub):
        for hd in range(hp):
            for r in range(jj, n_sub):
                step(hd, r, base + jj * sub, sub, jj == r)

    for hd in range(hp):
        acc = acc_scr[hd]
        o = acc[:, :LANES] / acc[:, LANES:]
        cols = slice(hd * LANES, (hd + 1) * LANES)
        o_ref[:, cols] = _rms(o, g_ref[:, cols]).astype(o_ref.dtype)


def _flash(q, k, v, gain, layer, g_off, *, batch, seq, name):
    heads, t, _ = q.shape
    tq, sub, hp = min(FLASH_TQ, seq), min(FLASH_SUB, seq), FLASH_HP
    nq = seq // tq
    assert heads % hp == 0 and g_off % hp == 0
    kv = pl.BlockSpec((hp, seq, ATT_W), lambda b, h, i: (h, b, 0))
    return pl.pallas_call(
        functools.partial(_flash_kernel, tq=tq, sub=sub, hp=hp),
        grid=(batch, heads // hp, nq),
        in_specs=[pl.BlockSpec((hp, tq, ATT_W), lambda b, h, i: (h, b * nq + i, 0)), kv, kv,
                  pl.BlockSpec((None, 1, hp * LANES), lambda b, h, i: (layer, 0, g_off // hp + h))],
        out_specs=pl.BlockSpec((tq, hp * LANES), lambda b, h, i: (b * nq + i, h)),
        out_shape=jax.ShapeDtypeStruct((t, heads * LANES), BF16),
        scratch_shapes=[pltpu.VMEM((hp, tq, LANES), F32), pltpu.VMEM((hp, tq, ATT_W), F32)],
        compiler_params=_cparams(("parallel", "parallel", "arbitrary"), 56),
        name=name,
    )(q, k, v, gain)


def _rope128(x, tab):
    c, s1, s2 = tab[:, :LANES], tab[:, LANES:2 * LANES], tab[:, 2 * LANES:]
    half = MLA_ROPE // 2
    return x * c + pltpu.roll(x, LANES - half, axis=1) * s1 + pltpu.roll(x, half, axis=1) * s2


def _mla_prep_kernel(tail_ref, ang_ref, gq_ref, gkv_ref, cqn_ref, ckvn_ref, krot_ref, tab_ref):
    tail = pltpu.roll(tail_ref[...], TAIL_W - TAIL_SHIFT, axis=1)
    cq = tail[:, :MLA_Q_LORA]
    ckv = tail[:, MLA_Q_LORA:MLA_Q_LORA + MLA_KV_LORA]
    misc = tail[:, MLA_Q_LORA + MLA_KV_LORA:]
    cqn_ref[...] = _rms(cq, gq_ref[...]).astype(cqn_ref.dtype)
    ckvn_ref[...] = _rms(ckv, gkv_ref[...]).astype(ckvn_ref.dtype)
    ang = ang_ref[...]
    cos, sin = jnp.cos(ang), jnp.sin(ang)
    lane = lax.broadcasted_iota(jnp.int32, ang.shape, 1)
    half = MLA_ROPE // 2
    tab = jnp.concatenate([
        jnp.where(lane < MLA_ROPE, cos, 0.0),
        jnp.where(lane < half, -sin, 0.0),
        jnp.where((lane >= half) & (lane < MLA_ROPE), sin, 0.0)], axis=1)
    tab_ref[...] = tab
    krot_ref[...] = _rope128(misc, tab).astype(krot_ref.dtype)


def _mla_prep(proj32, ang, gq, gkv, layer, *, tm=512):
    t = proj32.shape[0]
    tm = min(tm, t)
    return pl.pallas_call(
        _mla_prep_kernel,
        grid=(t // tm,),
        in_specs=[pl.BlockSpec((tm, TAIL_W), lambda i: (i, 0)),
                  pl.BlockSpec((tm, LANES), lambda i: (i, 0)),
                  pl.BlockSpec((None, 1, MLA_Q_LORA), lambda i: (layer, 0, 0)),
                  pl.BlockSpec((None, 1, MLA_KV_LORA), lambda i: (layer, 0, 0))],
        out_specs=[pl.BlockSpec((tm, MLA_Q_LORA), lambda i: (i, 0)),
                   pl.BlockSpec((tm, MLA_KV_LORA), lambda i: (i, 0)),
                   pl.BlockSpec((tm, LANES), lambda i: (i, 0)),
                   pl.BlockSpec((tm, 3 * LANES), lambda i: (i, 0))],
        out_shape=[jax.ShapeDtypeStruct((t, MLA_Q_LORA), BF16),
                   jax.ShapeDtypeStruct((t, MLA_KV_LORA), BF16),
                   jax.ShapeDtypeStruct((t, LANES), BF16),
                   jax.ShapeDtypeStruct((t, 3 * LANES), F32)],
        compiler_params=_cparams(("parallel",)),
        name="mla_prep",
    )(proj32, ang, gq, gkv)


def _mla_proj_kernel(cqn_ref, ckvn_ref, krot_ref, tab_ref, wq_ref, wkv_ref, q_ref, k_ref, v_ref):
    q = jnp.dot(cqn_ref[...], wq_ref[...], preferred_element_type=F32)
    q_ref[:, :LANES] = q[:, :LANES].astype(q_ref.dtype)
    q_ref[:, LANES:] = _rope128(q[:, LANES:], tab_ref[...]).astype(q_ref.dtype)
    kv = jnp.dot(ckvn_ref[...], wkv_ref[...], preferred_element_type=F32)
    k_ref[:, :LANES] = kv[:, :LANES].astype(k_ref.dtype)
    k_ref[:, LANES:] = krot_ref[...]
    v_ref[:, :LANES] = kv[:, LANES:].astype(v_ref.dtype)
    v_ref[:, LANES:] = jnp.ones((v_ref.shape[0], LANES), v_ref.dtype)


def _mla_proj(cqn, ckvn, krot, tab, wq, wkv, layer, *, tm=2048):
    t = cqn.shape[0]
    tm = min(tm, t)
    tok = lambda w: pl.BlockSpec((tm, w), lambda i, h: (i, 0))
    out = pl.BlockSpec((None, tm, ATT_W), lambda i, h: (h, i, 0))
    shp = jax.ShapeDtypeStruct((MLA_HEADS, t, ATT_W), BF16)
    return pl.pallas_call(
        _mla_proj_kernel,
        grid=(t // tm, MLA_HEADS),
        in_specs=[tok(MLA_Q_LORA), tok(MLA_KV_LORA), tok(LANES), tok(3 * LANES),
                  pl.BlockSpec((None, None, MLA_Q_LORA, ATT_W), lambda i, h: (layer, h, 0, 0)),
                  pl.BlockSpec((None, None, MLA_KV_LORA, 2 * LANES), lambda i, h: (layer, h, 0, 0))],
        out_specs=[out, out, out],
        out_shape=[shp, shp, shp],
        compiler_params=_cparams(("parallel", "parallel")),
        name="mla_proj",
    )(cqn, ckvn, krot, tab, wq, wkv)


def _mem_block_kernel(y_ref, h_ref, g0p_ref, g0n_ref, wq_ref, k_ref, v_ref, wo_ref, gp_ref, gn_ref,
                      ho_ref, hno_ref):
    scale = MEM_DIM ** -0.5
    h_mid = h_ref[...] + _rms(y_ref[...].astype(F32), g0p_ref[...])
    hn = _rms(h_mid, g0n_ref[...]).astype(BF16)
    q = jnp.dot(hn, wq_ref[...], preferred_element_type=F32).astype(BF16)
    outs = []
    for h in range(MEM_HEADS):
        sl = slice(h * MEM_DIM, (h + 1) * MEM_DIM)
        s = lax.dot_general(q[:, sl], k_ref[:, sl], (((1,), (1,)), ((), ())),
                            preferred_element_type=F32) * scale
        p = jnp.exp(s - jnp.max(s, axis=-1, keepdims=True))
        p = p / jnp.sum(p, axis=-1, keepdims=True)
        outs.append(jnp.dot(p.astype(BF16), v_ref[:, sl], preferred_element_type=F32))
    o = jnp.concatenate(outs, axis=1).astype(BF16)
    y = jnp.dot(o, wo_ref[...], preferred_element_type=F32)
    h_new = h_mid + _rms(y, gp_ref[...])
    ho_ref[...] = h_new
    hno_ref[...] = _rms(h_new, gn_ref[...]).astype(hno_ref.dtype)


def _mem_block(y, h, g0_post, g0_next, w_mq, km, vm, w_mo, g_post, g_next, layer, *, seq, mem_tokens, tm=256):
    t, d = h.shape
    tm = min(tm, seq)
    nt = seq // tm
    w = MEM_HEADS * MEM_DIM
    row = pl.BlockSpec((tm, d), lambda i: (i, 0))
    kv = pl.BlockSpec((mem_tokens, w), lambda i: (i // nt, 0))
    gain = pl.BlockSpec((None, 1, d), lambda i: (layer, 0, 0))
    return pl.pallas_call(
        _mem_block_kernel,
        grid=(t // tm,),
        in_specs=[row, row, gain, gain, pl.BlockSpec((None, d, w), lambda i: (layer, 0, 0)), kv, kv,
                  pl.BlockSpec((None, w, d), lambda i: (layer, 0, 0)), gain, gain],
        out_specs=[row, row],
        out_shape=[jax.ShapeDtypeStruct((t, d), F32), jax.ShapeDtypeStruct((t, d), BF16)],
        compiler_params=_cparams(("parallel",), 56),
        name="mem_block",
    )(y, h, g0_post, g0_next, w_mq, km, vm, w_mo, g_post, g_next)


def _split_w_in_kernel(w_ref, w16_ref, w32_ref):
    cast = lambda r0, r1: w_ref[r0:r1, :].astype(BF16)
    w16_ref[:1024, :] = cast(0, 1024)
    w16_ref[1024:3072, :] = cast(2048, 4096)
    w16_ref[3072:4096, :] = (w_ref[4096:5120, :] * (FOX_DIM ** -0.5 * LOG2E)).astype(BF16)
    w16_ref[4096:, :] = cast(5120, IN_TAIL0)
    n_tail, tc = w_ref.shape[0] - IN_TAIL0, w_ref.shape[1]
    tail = jnp.concatenate([w_ref[IN_TAIL0:, :], jnp.zeros((TAIL_W - n_tail, tc), F32)], axis=0)
    w32_ref[:TAIL_W, :] = tail.astype(BF16)
    w32_ref[TAIL_W:TAIL_W + 1024, :] = cast(1024, 2048)
    w32_ref[TAIL_W + 1024:, :] = jnp.zeros((P32_COLS - TAIL_W - 1024, tc), BF16)


def _split_w_in(w_in, *, tc=256):
    w_t = jnp.swapaxes(w_in, 1, 2)
    depth, cols, d = w_t.shape
    tc = min(tc, d)
    blk = lambda rows: pl.BlockSpec((None, rows, tc), lambda l, c: (l, 0, c))
    return pl.pallas_call(
        _split_w_in_kernel,
        grid=(depth, d // tc),
        in_specs=[blk(cols)],
        out_specs=[blk(P16_COLS), blk(P32_COLS)],
        out_shape=[jax.ShapeDtypeStruct((depth, P16_COLS, d), BF16),
                   jax.ShapeDtypeStruct((depth, P32_COLS, d), BF16)],
        compiler_params=_cparams(("parallel", "parallel"), 56),
        name="w_in_split",
    )(w_t)


def _split_mla_weights(w_uq, w_ukv):
    depth = w_uq.shape[0]
    wq = w_uq.reshape(depth, MLA_Q_LORA, MLA_HEADS, MLA_NOPE + MLA_ROPE).transpose(0, 2, 1, 3)
    wq = wq * ((MLA_NOPE + MLA_ROPE) ** -0.5 * LOG2E)
    wq = jnp.pad(wq, ((0, 0), (0, 0), (0, 0), (0, ATT_W - MLA_NOPE - MLA_ROPE)))
    wkv = w_ukv.reshape(depth, MLA_KV_LORA, MLA_HEADS, MLA_NOPE + MLA_V).transpose(0, 2, 1, 3)
    return wq.astype(BF16), wkv.astype(BF16)


def kernel(x, mem, positions, w_in, hg_lb_logits, fox_f_bias, mla_q_norm_g, mla_kv_norm_g, w_uq, w_ukv,
           mix_out_g, w_o, mem_norm_g, w_mq, w_mk, w_mv, w_mo, w_ff1, w_ff2, pre_mix_g, post_mix_g,
           pre_mem_g, post_mem_g, pre_ffn_g, post_ffn_g):
    batch, seq, d = x.shape
    depth = w_in.shape[0]
    t = batch * seq
    mem_tokens = mem.shape[1]

    w16, w32 = _split_w_in(w_in)
    wq, wkv = _split_mla_weights(w_uq, w_ukv)
    w_o16, w_mq16, w_mk16, w_mv16, w_mo16 = (w.astype(BF16) for w in (w_o, w_mq, w_mk, w_mv, w_mo))
    w_ff1_16, w_ff2_16 = w_ff1.astype(BF16), w_ff2.astype(BF16)
    row = lambda g: g.astype(F32).reshape(depth, 1, g.shape[-1])
    pre_mix, post_mix, pre_mem, post_mem, pre_ffn, post_ffn = map(
        row, (pre_mix_g, post_mix_g, pre_mem_g, post_mem_g, pre_ffn_g, post_ffn_g))
    mix_g, mem_g, gq, gkv = map(row, (mix_out_g, mem_norm_g, mla_q_norm_g, mla_kv_norm_g))
    fox_bias = jnp.pad(fox_f_bias.astype(F32), ((0, 0), (FF_LANE, LANES - FF_LANE - FOX_HEADS)))
    fox_bias = fox_bias.reshape(depth, 1, LANES)
    llb, l1m = _lower_bounds(hg_lb_logits)
    llb, l1m = row(llb), row(l1m)

    half = MLA_ROPE // 2
    inv = ROPE_THETA ** (-jnp.arange(half, dtype=F32) / half)
    ang = positions.reshape(t, 1).astype(F32) * inv[None, :]
    ang = jnp.concatenate([ang, ang, jnp.zeros((t, LANES - MLA_ROPE), F32)], axis=1)

    h = x.reshape(t, d)
    mem2 = mem.reshape(batch * mem_tokens, d)
    hn = _rms_cast(h, pre_mix, 0, name="rms_first")

    for l in range(depth):
        proj16 = _matmul(hn, w16, l, BF16, tm=1024, tn=1024, trans_b=True, name="in_proj16")
        proj32 = _matmul(hn, w32, l, F32, tm=1024, tn=P32_COLS // 2, trans_b=True, name="in_proj32")

        mix_a = _hgrn(proj16, proj32, llb, l1m, mix_g, l, batch=batch, seq=seq)

        fq, fk, fv = _fox_prep(proj16, proj32, fox_bias, l, batch=batch, seq=seq)
        mix_b = _flash(fq, fk, fv, mix_g, l, HG_HEADS, batch=batch, seq=seq, name="fox_attn")

        cqn, ckvn, krot, tab = _mla_prep(proj32, ang, gq, gkv, l)
        q_c, k_c, v_c = _mla_proj(cqn, ckvn, krot, tab, wq, wkv, l)
        mix_c = _flash(q_c, k_c, v_c, mix_g, l, HG_HEADS + FOX_HEADS, batch=batch, seq=seq, name="mla_attn")

        y = _matmul_parts([mix_a, mix_b, mix_c], w_o16, l, BF16, tm=1024, tn=1024, name="mix_out")

        mem_n = _rms_cast(mem2, mem_g, l, name="rms_mem")
        km = _matmul(mem_n, w_mk16, l, BF16, tm=512, tn=512, name="mem_k")
        vm = _matmul(mem_n, w_mv16, l, BF16, tm=512, tn=512, name="mem_v")
        h, hn = _mem_block(y, h, post_mix, pre_mem, w_mq16, km, vm, w_mo16, post_mem, pre_ffn, l,
                           seq=seq, mem_tokens=mem_tokens)

        u = _matmul(hn, w_ff1_16, l, BF16, tm=1024, tn=1024, relu2=True, name="ffn_up")
        y = _matmul(u, w_ff2_16, l, BF16, tm=1024, tn=1024, tk=4096, name="ffn_down")
        if l + 1 < depth:
            h, hn = _resid_norm(h, y, post_ffn, l, pre_mix, l + 1)
        else:
            h, _ = _resid_norm(h, y, post_ffn, l)

    return h.reshape(batch, seq, d)
```

```python
import functools

import jax
import jax.numpy as jnp
from jax import lax
from jax.experimental import pallas as pl
from jax.experimental.pallas import tpu as pltpu

F32 = jnp.float32
BF16 = jnp.bfloat16

D_MODEL = 4096
DEPTH = 4
HG_HEADS = 8
HG_DK = 128
HG_DV = 128
FOX_HEADS = 8
FOX_DIM = 128
MLA_HEADS = 16
MLA_Q_LORA = 768
MLA_KV_LORA = 512
MLA_NOPE = 128
MLA_ROPE = 64
MLA_V = 128
ROPE_THETA = 10000.0
MEM_HEADS = 4
MEM_DIM = 128
EPS = 1e-6
MASK_VALUE = -1e30
LB_FLOOR = 1e-30
LOG2E = 1.4426950408889634

LANES = 128
SUBLANES = 8
HG_CHUNK = 64
HG_SUB = 16
HG_UNROLL = 32
HG_BLOCK = 2048
ATT_W = 2 * LANES
FLASH_TQ = 2048
FLASH_SUB = 512
FLASH_HP = 2
FLASH_WIDE = 4

P16_HQ, P16_HI, P16_HG = 0, 8, 16
P16_FQKV = 3
P16_COLS = 6 * 1024
IN_TAIL0 = 7 * 1024
TAIL_W = 11 * LANES
TAIL_SHIFT = FOX_HEADS
P32_HF = 11
P32_COLS = 20 * LANES
FF_LANE = 0


def _cparams(sem, vmem_mb=None):
    kw = dict(dimension_semantics=sem)
    if vmem_mb is not None:
        kw["vmem_limit_bytes"] = vmem_mb * 1024 * 1024
    return pltpu.CompilerParams(**kw)


def _mm_kernel(a_ref, b_ref, o_ref, *scratch, nk, relu2, trans_b):
    def dot(a, b):
        dims = (((1,), (1,)), ((), ())) if trans_b else (((1,), (0,)), ((), ()))
        return lax.dot_general(a, b, dims, preferred_element_type=F32)

    def finish(r):
        if relu2:
            r = jnp.square(jnp.maximum(r, 0.0))
        o_ref[...] = r.astype(o_ref.dtype)

    if nk == 1:
        finish(dot(a_ref[...], b_ref[...]))
        return
    (acc_ref,) = scratch
    k = pl.program_id(2)

    @pl.when(k == 0)
    def _():
        acc_ref[...] = jnp.zeros_like(acc_ref)

    acc_ref[...] += dot(a_ref[...], b_ref[...])

    @pl.when(k == nk - 1)
    def _():
        finish(acc_ref[...])


def _matmul(a, w, layer, out_dtype, *, tm, tn, tk=None, relu2=False, trans_b=False, name="mm"):
    m, kdim = a.shape
    n = w.shape[-2] if trans_b else w.shape[-1]
    tk = kdim if tk is None else min(tk, kdim)
    tm, tn = min(tm, m), min(tn, n)
    assert m % tm == 0 and n % tn == 0 and kdim % tk == 0
    nk = kdim // tk
    scratch = [] if nk == 1 else [pltpu.VMEM((tm, tn), F32)]
    return pl.pallas_call(
        functools.partial(_mm_kernel, nk=nk, relu2=relu2, trans_b=trans_b),
        grid=(m // tm, n // tn, nk),
        in_specs=[
            pl.BlockSpec((tm, tk), lambda i, j, k: (i, k)),
            pl.BlockSpec((None, tn, tk), lambda i, j, k: (layer, j, k)) if trans_b else
            pl.BlockSpec((None, tk, tn), lambda i, j, k: (layer, k, j)),
        ],
        out_specs=pl.BlockSpec((tm, tn), lambda i, j, k: (i, j)),
        out_shape=jax.ShapeDtypeStruct((m, n), out_dtype),
        scratch_shapes=scratch,
        compiler_params=_cparams(("parallel", "parallel", "arbitrary"), 56),
        name=name,
    )(a, w)


def _mm_parts_kernel(*refs, widths):
    a_refs, b_ref, o_ref = refs[:len(widths)], refs[len(widths)], refs[len(widths) + 1]
    acc, off = None, 0
    for a_ref, wd in zip(a_refs, widths):
        r = jnp.dot(a_ref[...], b_ref[off:off + wd, :], preferred_element_type=F32)
        acc = r if acc is None else acc + r
        off += wd
    o_ref[...] = acc.astype(o_ref.dtype)


def _matmul_parts(parts, w, layer, out_dtype, *, tm, tn, name):
    m = parts[0].shape[0]
    widths = tuple(p.shape[1] for p in parts)
    kdim, n = sum(widths), w.shape[-1]
    tm, tn = min(tm, m), min(tn, n)
    assert m % tm == 0 and n % tn == 0 and w.shape[-2] == kdim
    return pl.pallas_call(
        functools.partial(_mm_parts_kernel, widths=widths),
        grid=(m // tm, n // tn),
        in_specs=[pl.BlockSpec((tm, wd), lambda i, j: (i, 0)) for wd in widths]
        + [pl.BlockSpec((None, kdim, tn), lambda i, j: (layer, 0, j))],
        out_specs=pl.BlockSpec((tm, tn), lambda i, j: (i, j)),
        out_shape=jax.ShapeDtypeStruct((m, n), out_dtype),
        compiler_params=_cparams(("parallel", "parallel"), 56),
        name=name,
    )(*parts, w)


def _rms(x, g):
    return x * lax.rsqrt(jnp.mean(x * x, axis=-1, keepdims=True) + EPS) * g


def _rms_cast_kernel(x_ref, g_ref, o_ref):
    o_ref[...] = _rms(x_ref[...].astype(F32), g_ref[...]).astype(o_ref.dtype)


def _rms_cast(x, g, layer, *, tm=256, name="rms_cast"):
    m, d = x.shape
    tm = min(tm, m)
    return pl.pallas_call(
        _rms_cast_kernel,
        grid=(m // tm,),
        in_specs=[pl.BlockSpec((tm, d), lambda i: (i, 0)),
                  pl.BlockSpec((None, 1, d), lambda i: (layer, 0, 0))],
        out_specs=pl.BlockSpec((tm, d), lambda i: (i, 0)),
        out_shape=jax.ShapeDtypeStruct((m, d), BF16),
        compiler_params=_cparams(("parallel",)),
        name=name,
    )(x, g)


def _resid_norm_kernel(h_ref, y_ref, gp_ref, *rest, with_next):
    h_new = h_ref[...] + _rms(y_ref[...].astype(F32), gp_ref[...])
    if with_next:
        gn_ref, ho_ref, hn_ref = rest
        hn_ref[...] = _rms(h_new, gn_ref[...]).astype(hn_ref.dtype)
    else:
        (ho_ref,) = rest
    ho_ref[...] = h_new


def _resid_norm(h, y, g_post, layer, g_next=None, layer_next=None, *, tm=256):
    m, d = h.shape
    tm = min(tm, m)
    row = pl.BlockSpec((tm, d), lambda i: (i, 0))
    with_next = g_next is not None
    in_specs = [row, row, pl.BlockSpec((None, 1, d), lambda i: (layer, 0, 0))]
    args = [h, y, g_post]
    out_shape = [jax.ShapeDtypeStruct((m, d), F32)]
    out_specs = [row]
    if with_next:
        in_specs.append(pl.BlockSpec((None, 1, d), lambda i: (layer_next, 0, 0)))
        args.append(g_next)
        out_shape.append(jax.ShapeDtypeStruct((m, d), BF16))
        out_specs.append(row)
    res = pl.pallas_call(
        functools.partial(_resid_norm_kernel, with_next=with_next),
        grid=(m // tm,),
        in_specs=in_specs,
        out_specs=out_specs,
        out_shape=out_shape,
        compiler_params=_cparams(("parallel",)),
        name="resid_norm",
    )(*args)
    return (res[0], res[1]) if with_next else (res[0], None)


def _lower_bound_kernel(x_ref, llb_ref, l1m_ref):
    depth = x_ref.shape[0]
    rows = [x_ref[i:i + 1, :].astype(F32) for i in range(depth)]
    mx = functools.reduce(jnp.maximum, rows)
    ex = [jnp.exp(r - mx) for r in rows]
    tot = functools.reduce(lambda a, b: a + b, ex)
    p = [e / tot for e in ex]
    cum = p[0]
    for i in range(depth):
        if i > 0:
            cum = cum + p[i]
        lb = jnp.clip(cum - p[0], 0.0, 1.0 - 1e-6)
        llb_ref[i:i + 1, :] = jnp.log(jnp.maximum(lb, LB_FLOOR))
        l1m_ref[i:i + 1, :] = jnp.log1p(-lb)


def _lower_bounds(logits):
    shp = jax.ShapeDtypeStruct(logits.shape, F32)
    return pl.pallas_call(_lower_bound_kernel, out_shape=[shp, shp], name="hgrn_lower_bounds")(logits)


def _hgrn_kernel(q_ref, v_ref, gate_ref, z_ref, llb_ref, l1m_ref, g_ref, o_ref,
                 state_ref, b_scr, k_scr, v_scr, *, n_chunks, unroll):
    c_len = HG_CHUNK
    n_rows = c_len // SUBLANES

    @pl.when(pl.program_id(2) == 0)
    def _():
        state_ref[...] = jnp.zeros_like(state_ref)

    lbf = jnp.exp(llb_ref[...])
    oml = jnp.exp(l1m_ref[...])
    gain = g_ref[...]
    r_i = lax.broadcasted_iota(jnp.int32, (c_len, c_len), 0)
    c_i = lax.broadcasted_iota(jnp.int32, (c_len, c_len), 1)
    tri = (c_i <= r_i).astype(F32)
    sub = lax.broadcasted_iota(jnp.int32, (SUBLANES, LANES), 0)
    takes = [(c_i >= jb * HG_SUB) & (c_i < (jb + 1) * HG_SUB) & (r_i >= (jb + 1) * HG_SUB)
             for jb in range(c_len // HG_SUB - 1)]

    def chunk(c, slot, state_t):
        rows = pl.ds(pl.multiple_of(c * c_len, c_len), c_len)
        z = z_ref[rows, :]
        q = q_ref[rows, :].astype(F32)
        v16 = v_ref[rows, :]
        e = jnp.exp(-jnp.abs(z))
        r = 1.0 / (1.0 + e)
        er = e * r
        pos = z >= 0.0
        log_f = jnp.log(lbf + oml * jnp.where(pos, r, er))
        kk = oml * jnp.where(pos, er, r)
        b = jnp.dot(tri, log_f, preferred_element_type=F32, precision=lax.Precision.HIGHEST) * LOG2E
        b_scr[slot] = b
        k_scr[slot] = kk
        v_scr[slot] = v16.astype(F32)

        q_rows = [q[r * SUBLANES:(r + 1) * SUBLANES, :] for r in range(n_rows)]
        b_rows = [b[r * SUBLANES:(r + 1) * SUBLANES, :] for r in range(n_rows)]
        o_rows = [jnp.zeros((SUBLANES, LANES), F32) for _ in range(n_rows)]
        rows_per_sb = HG_SUB // SUBLANES
        for g in range(n_rows):
            r_hi = (g // rows_per_sb + 1) * rows_per_sb
            for i in range(SUBLANES):
                s = g * SUBLANES + i
                bs = b_scr[slot, s:s + 1, :]
                ks = k_scr[slot, s:s + 1, :]
                vs = v_scr[slot, s:s + 1, :]
                for r in range(g, r_hi):
                    e = jnp.exp2(b_rows[r] - bs)
                    if r == g and i > 0:
                        e = jnp.where(sub >= i, e, 0.0)
                    p = e * (q_rows[r] * ks)
                    o_rows[r] = o_rows[r] + jnp.sum(p, axis=-1, keepdims=True) * vs
        o = jnp.concatenate(o_rows, axis=0)

        n_sb = c_len // HG_SUB
        bend = [b[(jb + 1) * HG_SUB - 1:(jb + 1) * HG_SUB, :] for jb in range(n_sb)]
        bend_rows = jnp.concatenate([jnp.broadcast_to(e_, (HG_SUB, LANES)) for e_ in bend], axis=0)
        khat = (kk * jnp.exp2(bend_rows - b)).astype(BF16)
        a_stack = jnp.concatenate(
            [q * jnp.exp2(jnp.minimum(b - bend[jb], 0.0)) for jb in range(n_sb - 1)], axis=0).astype(BF16)
        r_all = lax.dot_general(a_stack, khat, (((1,), (1,)), ((), ())), preferred_element_type=F32)
        s_off = jnp.zeros((c_len, c_len), F32)
        for jb in range(n_sb - 1):
            s_off = jnp.where(takes[jb], r_all[jb * c_len:(jb + 1) * c_len, :], s_off)
        o = o + jnp.dot(s_off.astype(BF16), v16, preferred_element_type=F32)

        qe = (q * jnp.exp2(b)).astype(BF16)
        o = o + lax.dot_general(qe, state_t.astype(BF16), (((1,), (1,)), ((), ())),
                                preferred_element_type=F32)
        b_end = b[c_len - 1:c_len, :]
        kd = (kk * jnp.exp2(b_end - b)).astype(BF16)
        upd = lax.dot_general(v16, kd, (((0,), (0,)), ((), ())), preferred_element_type=F32)
        state_t = state_t * jnp.exp2(b_end) + upd

        gate = gate_ref[rows, :].astype(F32)
        y = _rms(o, gain) * (gate * jax.nn.sigmoid(gate))
        o_ref[rows, :] = y.astype(o_ref.dtype)
        return state_t

    def trip(cc, state_t):
        for slot in range(unroll):
            state_t = chunk(unroll * cc + slot, slot, state_t)
        return state_t

    state_ref[...] = lax.fori_loop(0, n_chunks // unroll, trip, state_ref[...])


def _hgrn(proj16, proj32, llb, l1m, mix_g, layer, *, batch, seq, blk=HG_BLOCK):
    blk = min(blk, seq)
    ns = seq // blk
    unroll = min(HG_UNROLL, blk // HG_CHUNK)
    assert (blk // HG_CHUNK) % unroll == 0
    tok = lambda col0: pl.BlockSpec((blk, LANES), lambda b, h, s: (b * ns + s, col0 + h))
    par = pl.BlockSpec((None, 1, LANES), lambda b, h, s: (layer, 0, h))
    return pl.pallas_call(
        functools.partial(_hgrn_kernel, n_chunks=blk // HG_CHUNK, unroll=unroll),
        grid=(batch, HG_HEADS, ns),
        in_specs=[tok(P16_HQ), tok(P16_HI), tok(P16_HG), tok(P32_HF), par, par, par],
        out_specs=pl.BlockSpec((blk, LANES), lambda b, h, s: (b * ns + s, h)),
        out_shape=jax.ShapeDtypeStruct((batch * seq, HG_HEADS * HG_DV), BF16),
        scratch_shapes=[pltpu.VMEM((HG_DV, HG_DK), F32)] + [pltpu.VMEM((unroll, HG_CHUNK, LANES), F32)] * 3,
        compiler_params=_cparams(("parallel", "parallel", "arbitrary")),
        name="hgrn2_scan",
    )(proj16, proj16, proj16, proj32, llb, l1m, mix_g)


def _fox_prep_kernel(x_ref, bias_ref, fq_ref, fk_ref, fv_ref, qo_ref, ko_ref, vo_ref, carry_ref, *, blk):
    @pl.when(pl.program_id(1) == 0)
    def _():
        carry_ref[...] = jnp.zeros_like(carry_ref)

    x = x_ref[...] + bias_ref[...]
    lf = jnp.minimum(x, 0.0) - jnp.log1p(jnp.exp(-jnp.abs(x)))
    r_i = lax.broadcasted_iota(jnp.int32, (blk, blk), 0)
    c_i = lax.broadcasted_iota(jnp.int32, (blk, blk), 1)
    tri = (c_i <= r_i).astype(F32)
    cum = jnp.dot(tri, lf, preferred_element_type=F32, precision=lax.Precision.HIGHEST) + carry_ref[...]
    carry_ref[...] = cum[blk - 1:blk, :]
    c2 = cum * LOG2E
    lane = lax.broadcasted_iota(jnp.int32, (blk, LANES), 1)
    ones = jnp.ones((blk, LANES), BF16)
    for h in range(FOX_HEADS):
        col = jnp.broadcast_to(c2[:, FF_LANE + h:FF_LANE + h + 1], (blk, LANES))
        hi = col.astype(BF16).astype(F32)
        r1 = col - hi
        mid = r1.astype(BF16).astype(F32)
        lo = r1 - mid
        pieces = jnp.where((lane == 0) | (lane == 3), hi, jnp.where((lane == 1) | (lane == 4), mid, lo))
        q_ext = jnp.where(lane < 3, pieces, jnp.where(lane < 6, 1.0, 0.0))
        k_ext = jnp.where(lane < 3, 1.0, jnp.where(lane < 6, -pieces, 0.0))
        sl = slice(h * LANES, (h + 1) * LANES)
        qo_ref[h, :, :LANES] = fq_ref[:, sl]
        qo_ref[h, :, LANES:] = q_ext.astype(BF16)
        ko_ref[h, :, :LANES] = fk_ref[:, sl]
        ko_ref[h, :, LANES:] = k_ext.astype(BF16)
        vo_ref[h, :, :LANES] = fv_ref[:, sl]
        vo_ref[h, :, LANES:] = ones


def _fox_prep(proj16, proj32, bias_row, layer, *, batch, seq, blk=256):
    blk = min(blk, seq)
    nb = seq // blk
    t = batch * seq
    w = FOX_HEADS * FOX_DIM
    tok = lambda c: pl.BlockSpec((blk, w), lambda b, j: (b * nb + j, P16_FQKV + c))
    out = pl.BlockSpec((FOX_HEADS, blk, ATT_W), lambda b, j: (0, b * nb + j, 0))
    shp = jax.ShapeDtypeStruct((FOX_HEADS, t, ATT_W), BF16)
    return pl.pallas_call(
        functools.partial(_fox_prep_kernel, blk=blk),
        grid=(batch, nb),
        in_specs=[pl.BlockSpec((blk, LANES), lambda b, j: (b * nb + j, 0)),
                  pl.BlockSpec((None, 1, LANES), lambda b, j: (layer, 0, 0)),
                  tok(0), tok(1), tok(2)],
        out_specs=[out, out, out],
        out_shape=[shp, shp, shp],
        scratch_shapes=[pltpu.VMEM((1, LANES), F32)],
        compiler_params=_cparams(("parallel", "arbitrary")),
        name="fox_prep",
    )(proj32, bias_row, proj16, proj16, proj16)


def _flash_kernel(q_ref, k_ref, v_ref, g_ref, o_ref, m_scr, acc_scr, *, tq, sub, hp):
    i = pl.program_id(2)
    n_sub = tq // sub
    m_scr[...] = jnp.full_like(m_scr, MASK_VALUE)
    acc_scr[...] = jnp.zeros_like(acc_scr)
    r_i = lax.broadcasted_iota(jnp.int32, (sub, sub), 0)
    c_i = lax.broadcasted_iota(jnp.int32, (sub, sub), 1)

    def step(hd, r, kv0, n_keys, diagonal):
        rows = slice(r * sub, (r + 1) * sub)
        keys = pl.ds(kv0, n_keys)
        s = lax.dot_general(q_ref[hd, rows, :], k_ref[hd, keys, :], (((1,), (1,)), ((), ())),
                            preferred_element_type=F32)
        if diagonal:
            s = jnp.where(c_i <= r_i, s, MASK_VALUE)
        m_prev = m_scr[hd, rows, :]
        m_new = jnp.maximum(m_prev, jnp.max(s, axis=-1, keepdims=True))
        alpha = jnp.exp2(m_prev - m_new)
        p = jnp.exp2(s - jnp.concatenate([m_new] * (n_keys // LANES), axis=1)).astype(BF16)
        pv = jnp.dot(p, v_ref[hd, keys, :], preferred_element_type=F32)
        acc_scr[hd, rows, :] = jnp.concatenate([alpha, alpha], axis=1) * acc_scr[hd, rows, :] + pv
        m_scr[hd, rows, :] = m_new

    wide = FLASH_WIDE * sub

    def full_chunks(j, carry):
        kv0 = pl.multiple_of(j * wide, wide)
        for hd in range(hp):
            for r in range(n_sub):
                step(hd, r, kv0, wide, False)
        return carry

    assert n_sub % FLASH_WIDE == 0
    lax.fori_loop(0, i * n_sub // FLASH_WIDE, full_chunks, 0)
    base = pl.multiple_of(i * tq, tq)
    for jj in range(n_sub):
        for hd in range(hp):
            for r in range(jj, n_sub):
                step(hd, r, base + jj * sub, sub, jj == r)

    for hd in range(hp):
        acc = acc_scr[hd]
        o = acc[:, :LANES] / acc[:, LANES:]
        cols = slice(hd * LANES, (hd + 1) * LANES)
        o_ref[:, cols] = _rms(o, g_ref[:, cols]).astype(o_ref.dtype)


def _flash(q, k, v, gain, layer, g_off, *, batch, seq, name):
    heads, t, _ = q.shape
    tq, sub, hp = min(FLASH_TQ, seq), min(FLASH_SUB, seq), FLASH_HP
    nq = seq // tq
    assert heads % hp == 0 and g_off % hp == 0
    kv = pl.BlockSpec((hp, seq, ATT_W), lambda b, h, i: (h, b, 0))
    return pl.pallas_call(
        functools.partial(_flash_kernel, tq=tq, sub=sub, hp=hp),
        grid=(batch, heads // hp, nq),
        in_specs=[pl.BlockSpec((hp, tq, ATT_W), lambda b, h, i: (h, b * nq + i, 0)), kv, kv,
                  pl.BlockSpec((None, 1, hp * LANES), lambda b, h, i: (layer, 0, g_off // hp + h))],
        out_specs=pl.BlockSpec((tq, hp * LANES), lambda b, h, i: (b * nq + i, h)),
        out_shape=jax.ShapeDtypeStruct((t, heads * LANES), BF16),
        scratch_shapes=[pltpu.VMEM((hp, tq, LANES), F32), pltpu.VMEM((hp, tq, ATT_W), F32)],
        compiler_params=_cparams(("parallel", "parallel", "arbitrary"), 56),
        name=name,
    )(q, k, v, gain)


def _rope128(x, tab):
    c, s1, s2 = tab[:, :LANES], tab[:, LANES:2 * LANES], tab[:, 2 * LANES:]
    half = MLA_ROPE // 2
    return x * c + pltpu.roll(x, LANES - half, axis=1) * s1 + pltpu.roll(x, half, axis=1) * s2


def _mla_prep_kernel(tail_ref, ang_ref, gq_ref, gkv_ref, cqn_ref, ckvn_ref, krot_ref, tab_ref):
    tail = pltpu.roll(tail_ref[...], TAIL_W - TAIL_SHIFT, axis=1)
    cq = tail[:, :MLA_Q_LORA]
    ckv = tail[:, MLA_Q_LORA:MLA_Q_LORA + MLA_KV_LORA]
    misc = tail[:, MLA_Q_LORA + MLA_KV_LORA:]
    cqn_ref[...] = _rms(cq, gq_ref[...]).astype(cqn_ref.dtype)
    ckvn_ref[...] = _rms(ckv, gkv_ref[...]).astype(ckvn_ref.dtype)
    ang = ang_ref[...]
    cos, sin = jnp.cos(ang), jnp.sin(ang)
    lane = lax.broadcasted_iota(jnp.int32, ang.shape, 1)
    half = MLA_ROPE // 2
    tab = jnp.concatenate([
        jnp.where(lane < MLA_ROPE, cos, 0.0),
        jnp.where(lane < half, -sin, 0.0),
        jnp.where((lane >= half) & (lane < MLA_ROPE), sin, 0.0)], axis=1)
    tab_ref[...] = tab
    krot_ref[...] = _rope128(misc, tab).astype(krot_ref.dtype)


def _mla_prep(proj32, ang, gq, gkv, layer, *, tm=512):
    t = proj32.shape[0]
    tm = min(tm, t)
    return pl.pallas_call(
        _mla_prep_kernel,
        grid=(t // tm,),
        in_specs=[pl.BlockSpec((tm, TAIL_W), lambda i: (i, 0)),
                  pl.BlockSpec((tm, LANES), lambda i: (i, 0)),
                  pl.BlockSpec((None, 1, MLA_Q_LORA), lambda i: (layer, 0, 0)),
                  pl.BlockSpec((None, 1, MLA_KV_LORA), lambda i: (layer, 0, 0))],
        out_specs=[pl.BlockSpec((tm, MLA_Q_LORA), lambda i: (i, 0)),
                   pl.BlockSpec((tm, MLA_KV_LORA), lambda i: (i, 0)),
                   pl.BlockSpec((tm, LANES), lambda i: (i, 0)),
                   pl.BlockSpec((tm, 3 * LANES), lambda i: (i, 0))],
        out_shape=[jax.ShapeDtypeStruct((t, MLA_Q_LORA), BF16),
                   jax.ShapeDtypeStruct((t, MLA_KV_LORA), BF16),
                   jax.ShapeDtypeStruct((t, LANES), BF16),
                   jax.ShapeDtypeStruct((t, 3 * LANES), F32)],
        compiler_params=_cparams(("parallel",)),
        name="mla_prep",
    )(proj32, ang, gq, gkv)


def _mla_proj_kernel(cqn_ref, ckvn_ref, krot_ref, tab_ref, wq_ref, wkv_ref, q_ref, k_ref, v_ref):
    q = jnp.dot(cqn_ref[...], wq_ref[...], preferred_element_type=F32)
    q_ref[:, :LANES] = q[:, :LANES].astype(q_ref.dtype)
    q_ref[:, LANES:] = _rope128(q[:, LANES:], tab_ref[...]).astype(q_ref.dtype)
    kv = jnp.dot(ckvn_ref[...], wkv_ref[...], preferred_element_type=F32)
    k_ref[:, :LANES] = kv[:, :LANES].astype(k_ref.dtype)
    k_ref[:, LANES:] = krot_ref[...]
    v_ref[:, :LANES] = kv[:, LANES:].astype(v_ref.dtype)
    v_ref[:, LANES:] = jnp.ones((v_ref.shape[0], LANES), v_ref.dtype)


def _mla_proj(cqn, ckvn, krot, tab, wq, wkv, layer, *, tm=2048):
    t = cqn.shape[0]
    tm = min(tm, t)
    tok = lambda w: pl.BlockSpec((tm, w), lambda i, h: (i, 0))
    out = pl.BlockSpec((None, tm, ATT_W), lambda i, h: (h, i, 0))
    shp = jax.ShapeDtypeStruct((MLA_HEADS, t, ATT_W), BF16)
    return pl.pallas_call(
        _mla_proj_kernel,
        grid=(t // tm, MLA_HEADS),
        in_specs=[tok(MLA_Q_LORA), tok(MLA_KV_LORA), tok(LANES), tok(3 * LANES),
                  pl.BlockSpec((None, None, MLA_Q_LORA, ATT_W), lambda i, h: (layer, h, 0, 0)),
                  pl.BlockSpec((None, None, MLA_KV_LORA, 2 * LANES), lambda i, h: (layer, h, 0, 0))],
        out_specs=[out, out, out],
        out_shape=[shp, shp, shp],
        compiler_params=_cparams(("parallel", "parallel")),
        name="mla_proj",
    )(cqn, ckvn, krot, tab, wq, wkv)


def _mem_block_kernel(y_ref, h_ref, g0p_ref, g0n_ref, wq_ref, k_ref, v_ref, wo_ref, gp_ref, gn_ref,
                      ho_ref, hno_ref):
    scale = MEM_DIM ** -0.5
    h_mid = h_ref[...] + _rms(y_ref[...].astype(F32), g0p_ref[...])
    hn = _rms(h_mid, g0n_ref[...]).astype(BF16)
    q = jnp.dot(hn, wq_ref[...], preferred_element_type=F32).astype(BF16)
    outs = []
    for h in range(MEM_HEADS):
        sl = slice(h * MEM_DIM, (h + 1) * MEM_DIM)
        s = lax.dot_general(q[:, sl], k_ref[:, sl], (((1,), (1,)), ((), ())),
                            preferred_element_type=F32) * scale
        p = jnp.exp(s - jnp.max(s, axis=-1, keepdims=True))
        p = p / jnp.sum(p, axis=-1, keepdims=True)
        outs.append(jnp.dot(p.astype(BF16), v_ref[:, sl], preferred_element_type=F32))
    o = jnp.concatenate(outs, axis=1).astype(BF16)
    y = jnp.dot(o, wo_ref[...], preferred_element_type=F32)
    h_new = h_mid + _rms(y, gp_ref[...])
    ho_ref[...] = h_new
    hno_ref[...] = _rms(h_new, gn_ref[...]).astype(hno_ref.dtype)


def _mem_block(y, h, g0_post, g0_next, w_mq, km, vm, w_mo, g_post, g_next, layer, *, seq, mem_tokens, tm=256):
    t, d = h.shape
    tm = min(tm, seq)
    nt = seq // tm
    w = MEM_HEADS * MEM_DIM
    row = pl.BlockSpec((tm, d), lambda i: (i, 0))
    kv = pl.BlockSpec((mem_tokens, w), lambda i: (i // nt, 0))
    gain = pl.BlockSpec((None, 1, d), lambda i: (layer, 0, 0))
    return pl.pallas_call(
        _mem_block_kernel,
        grid=(t // tm,),
        in_specs=[row, row, gain, gain, pl.BlockSpec((None, d, w), lambda i: (layer, 0, 0)), kv, kv,
                  pl.BlockSpec((None, w, d), lambda i: (layer, 0, 0)), gain, gain],
        out_specs=[row, row],
        out_shape=[jax.ShapeDtypeStruct((t, d), F32), jax.ShapeDtypeStruct((t, d), BF16)],
        compiler_params=_cparams(("parallel",), 56),
        name="mem_block",
    )(y, h, g0_post, g0_next, w_mq, km, vm, w_mo, g_post, g_next)


def _split_w_in_kernel(w_ref, w16_ref, w32_ref):
    cast = lambda r0, r1: w_ref[r0:r1, :].astype(BF16)
    w16_ref[:1024, :] = cast(0, 1024)
    w16_ref[1024:3072, :] = cast(2048, 4096)
    w16_ref[3072:4096, :] = (w_ref[4096:5120, :] * (FOX_DIM ** -0.5 * LOG2E)).astype(BF16)
    w16_ref[4096:, :] = cast(5120, IN_TAIL0)
    n_tail, tc = w_ref.shape[0] - IN_TAIL0, w_ref.shape[1]
    tail = jnp.concatenate([w_ref[IN_TAIL0:, :], jnp.zeros((TAIL_W - n_tail, tc), F32)], axis=0)
    w32_ref[:TAIL_W, :] = tail.astype(BF16)
    w32_ref[TAIL_W:TAIL_W + 1024, :] = cast(1024, 2048)
    w32_ref[TAIL_W + 1024:, :] = jnp.zeros((P32_COLS - TAIL_W - 1024, tc), BF16)


def _split_w_in(w_in, *, tc=256):
    w_t = jnp.swapaxes(w_in, 1, 2)
    depth, cols, d = w_t.shape
    tc = min(tc, d)
    blk = lambda rows: pl.BlockSpec((None, rows, tc), lambda l, c: (l, 0, c))
    return pl.pallas_call(
        _split_w_in_kernel,
        grid=(depth, d // tc),
        in_specs=[blk(cols)],
        out_specs=[blk(P16_COLS), blk(P32_COLS)],
        out_shape=[jax.ShapeDtypeStruct((depth, P16_COLS, d), BF16),
                   jax.ShapeDtypeStruct((depth, P32_COLS, d), BF16)],
        compiler_params=_cparams(("parallel", "parallel"), 56),
        name="w_in_split",
    )(w_t)


def _split_mla_weights(w_uq, w_ukv):
    depth = w_uq.shape[0]
    wq = w_uq.reshape(depth, MLA_Q_LORA, MLA_HEADS, MLA_NOPE + MLA_ROPE).transpose(0, 2, 1, 3)
    wq = wq * ((MLA_NOPE + MLA_ROPE) ** -0.5 * LOG2E)
    wq = jnp.pad(wq, ((0, 0), (0, 0), (0, 0), (0, ATT_W - MLA_NOPE - MLA_ROPE)))
    wkv = w_ukv.reshape(depth, MLA_KV_LORA, MLA_HEADS, MLA_NOPE + MLA_V).transpose(0, 2, 1, 3)
    return wq.astype(BF16), wkv.astype(BF16)


def kernel(x, mem, positions, w_in, hg_lb_logits, fox_f_bias, mla_q_norm_g, mla_kv_norm_g, w_uq, w_ukv,
           mix_out_g, w_o, mem_norm_g, w_mq, w_mk, w_mv, w_mo, w_ff1, w_ff2, pre_mix_g, post_mix_g,
           pre_mem_g, post_mem_g, pre_ffn_g, post_ffn_g):
    batch, seq, d = x.shape
    depth = w_in.shape[0]
    t = batch * seq
    mem_tokens = mem.shape[1]

    w16, w32 = _split_w_in(w_in)
    wq, wkv = _split_mla_weights(w_uq, w_ukv)
    w_o16, w_mq16, w_mk16, w_mv16, w_mo16 = (w.astype(BF16) for w in (w_o, w_mq, w_mk, w_mv, w_mo))
    w_ff1_16, w_ff2_16 = w_ff1.astype(BF16), w_ff2.astype(BF16)
    row = lambda g: g.astype(F32).reshape(depth, 1, g.shape[-1])
    pre_mix, post_mix, pre_mem, post_mem, pre_ffn, post_ffn = map(
        row, (pre_mix_g, post_mix_g, pre_mem_g, post_mem_g, pre_ffn_g, post_ffn_g))
    mix_g, mem_g, gq, gkv = map(row, (mix_out_g, mem_norm_g, mla_q_norm_g, mla_kv_norm_g))
    fox_bias = jnp.pad(fox_f_bias.astype(F32), ((0, 0), (FF_LANE, LANES - FF_LANE - FOX_HEADS)))
    fox_bias = fox_bias.reshape(depth, 1, LANES)
    llb, l1m = _lower_bounds(hg_lb_logits)
    llb, l1m = row(llb), row(l1m)

    half = MLA_ROPE // 2
    inv = ROPE_THETA ** (-jnp.arange(half, dtype=F32) / half)
    ang = positions.reshape(t, 1).astype(F32) * inv[None, :]
    ang = jnp.concatenate([ang, ang, jnp.zeros((t, LANES - MLA_ROPE), F32)], axis=1)

    h = x.reshape(t, d)
    mem2 = mem.reshape(batch * mem_tokens, d)
    hn = _rms_cast(h, pre_mix, 0, name="rms_first")

    for l in range(depth):
        proj16 = _matmul(hn, w16, l, BF16, tm=1024, tn=1024, trans_b=True, name="in_proj16")
        proj32 = _matmul(hn, w32, l, F32, tm=1024, tn=P32_COLS // 2, trans_b=True, name="in_proj32")

        mix_a = _hgrn(proj16, proj32, llb, l1m, mix_g, l, batch=batch, seq=seq)

        fq, fk, fv = _fox_prep(proj16, proj32, fox_bias, l, batch=batch, seq=seq)
        mix_b = _flash(fq, fk, fv, mix_g, l, HG_HEADS, batch=batch, seq=seq, name="fox_attn")

        cqn, ckvn, krot, tab = _mla_prep(proj32, ang, gq, gkv, l)
        q_c, k_c, v_c = _mla_proj(cqn, ckvn, krot, tab, wq, wkv, l)
        mix_c = _flash(q_c, k_c, v_c, mix_g, l, HG_HEADS + FOX_HEADS, batch=batch, seq=seq, name="mla_attn")

        y = _matmul_parts([mix_a, mix_b, mix_c], w_o16, l, BF16, tm=1024, tn=1024, name="mix_out")

        mem_n = _rms_cast(mem2, mem_g, l, name="rms_mem")
        km = _matmul(mem_n, w_mk16, l, BF16, tm=512, tn=512, name="mem_k")
        vm = _matmul(mem_n, w_mv16, l, BF16, tm=512, tn=512, name="mem_v")
        h, hn = _mem_block(y, h, post_mix, pre_mem, w_mq16, km, vm, w_mo16, post_mem, pre_ffn, l,
                           seq=seq, mem_tokens=mem_tokens)

        u = _matmul(hn, w_ff1_16, l, BF16, tm=1024, tn=1024, relu2=True, name="ffn_up")
        y = _matmul(u, w_ff2_16, l, BF16, tm=1024, tn=1024, tk=4096, name="ffn_down")
        if l + 1 < depth:
            h, hn = _resid_norm(h, y, post_ffn, l, pre_mix, l + 1)
        else:
            h, _ = _resid_norm(h, y, post_ffn, l)

    return h.reshape(batch, seq, d)
```

```python
import functools

import jax
import jax.numpy as jnp
from jax import lax
from jax.experimental import pallas as pl
from jax.experimental.pallas import tpu as pltpu

F32 = jnp.float32
BF16 = jnp.bfloat16

HG_HEADS = 8
HG_DK = 128
HG_DV = 128
FOX_HEADS = 8
FOX_DIM = 128
MLA_HEADS = 16
MLA_Q_LORA = 768
MLA_KV_LORA = 512
MLA_NOPE = 128
MLA_ROPE = 64
MLA_V = 128
ROPE_THETA = 10000.0
MEM_HEADS = 4
MEM_DIM = 128
EPS = 1e-6
MASK_VALUE = -1e30
LB_FLOOR = 1e-30
LOG2E = 1.4426950408889634

LANES = 128
SUBLANES = 8
VMEM_BIG_MB = 56
HG_CHUNK = 64
HG_SUB = 16
HG_UNROLL = 32
HG_BLOCK = 2048
ATT_W = 2 * LANES
FLASH_TQ = 2048
FLASH_SUB = 512
FLASH_HP = 2
FLASH_WIDE = 4

P16_HQ, P16_HI, P16_HG = 0, 8, 16
P16_FQKV = 3
P16_COLS = 6 * 1024
IN_TAIL0 = 7 * 1024
TAIL_W = 11 * LANES
TAIL_SHIFT = FOX_HEADS
P32_HF = 11
P32_COLS = 20 * LANES
FF_LANE = 0


def _cparams(sem, vmem_mb=None):
    kw = dict(dimension_semantics=sem)
    if vmem_mb is not None:
        kw["vmem_limit_bytes"] = vmem_mb * 1024 * 1024
    return pltpu.CompilerParams(**kw)


def _mm_kernel(a_ref, b_ref, o_ref, *scratch, nk, relu2, trans_b):
    def dot(a, b):
        dims = (((1,), (1,)), ((), ())) if trans_b else (((1,), (0,)), ((), ()))
        return lax.dot_general(a, b, dims, preferred_element_type=F32)

    def finish(r):
        if relu2:
            r = jnp.square(jnp.maximum(r, 0.0))
        o_ref[...] = r.astype(o_ref.dtype)

    if nk == 1:
        finish(dot(a_ref[...], b_ref[...]))
        return
    (acc_ref,) = scratch
    k = pl.program_id(2)

    @pl.when(k == 0)
    def _():
        acc_ref[...] = jnp.zeros_like(acc_ref)

    acc_ref[...] += dot(a_ref[...], b_ref[...])

    @pl.when(k == nk - 1)
    def _():
        finish(acc_ref[...])


def _matmul(a, w, layer, out_dtype, *, tm, tn, tk=None, relu2=False, trans_b=False, name="mm"):
    m, kdim = a.shape
    n = w.shape[-2] if trans_b else w.shape[-1]
    tk = kdim if tk is None else min(tk, kdim)
    tm, tn = min(tm, m), min(tn, n)
    assert m % tm == 0 and n % tn == 0 and kdim % tk == 0
    nk = kdim // tk
    scratch = [] if nk == 1 else [pltpu.VMEM((tm, tn), F32)]
    return pl.pallas_call(
        functools.partial(_mm_kernel, nk=nk, relu2=relu2, trans_b=trans_b),
        grid=(m // tm, n // tn, nk),
        in_specs=[
            pl.BlockSpec((tm, tk), lambda i, j, k: (i, k)),
            pl.BlockSpec((None, tn, tk), lambda i, j, k: (layer, j, k)) if trans_b else
            pl.BlockSpec((None, tk, tn), lambda i, j, k: (layer, k, j)),
        ],
        out_specs=pl.BlockSpec((tm, tn), lambda i, j, k: (i, j)),
        out_shape=jax.ShapeDtypeStruct((m, n), out_dtype),
        scratch_shapes=scratch,
        compiler_params=_cparams(("parallel", "parallel", "arbitrary"), VMEM_BIG_MB),
        name=name,
    )(a, w)


def _mm_parts_kernel(*refs, widths):
    a_refs, b_ref, o_ref = refs[:len(widths)], refs[len(widths)], refs[len(widths) + 1]
    acc, off = None, 0
    for a_ref, wd in zip(a_refs, widths):
        r = jnp.dot(a_ref[...], b_ref[off:off + wd, :], preferred_element_type=F32)
        acc = r if acc is None else acc + r
        off += wd
    o_ref[...] = acc.astype(o_ref.dtype)


def _matmul_parts(parts, w, layer, out_dtype, *, tm, tn, name):
    m = parts[0].shape[0]
    widths = tuple(p.shape[1] for p in parts)
    kdim, n = sum(widths), w.shape[-1]
    tm, tn = min(tm, m), min(tn, n)
    assert m % tm == 0 and n % tn == 0 and w.shape[-2] == kdim
    return pl.pallas_call(
        functools.partial(_mm_parts_kernel, widths=widths),
        grid=(m // tm, n // tn),
        in_specs=[pl.BlockSpec((tm, wd), lambda i, j: (i, 0)) for wd in widths]
        + [pl.BlockSpec((None, kdim, tn), lambda i, j: (layer, 0, j))],
        out_specs=pl.BlockSpec((tm, tn), lambda i, j: (i, j)),
        out_shape=jax.ShapeDtypeStruct((m, n), out_dtype),
        compiler_params=_cparams(("parallel", "parallel"), VMEM_BIG_MB),
        name=name,
    )(*parts, w)


def _rms(x, g):
    return x * lax.rsqrt(jnp.mean(x * x, axis=-1, keepdims=True) + EPS) * g


def _rms_cast_kernel(x_ref, g_ref, o_ref):
    o_ref[...] = _rms(x_ref[...].astype(F32), g_ref[...]).astype(o_ref.dtype)


def _rms_cast(x, g, layer, *, tm=256, name="rms_cast"):
    m, d = x.shape
    tm = min(tm, m)
    return pl.pallas_call(
        _rms_cast_kernel,
        grid=(m // tm,),
        in_specs=[pl.BlockSpec((tm, d), lambda i: (i, 0)),
                  pl.BlockSpec((None, 1, d), lambda i: (layer, 0, 0))],
        out_specs=pl.BlockSpec((tm, d), lambda i: (i, 0)),
        out_shape=jax.ShapeDtypeStruct((m, d), BF16),
        compiler_params=_cparams(("parallel",)),
        name=name,
    )(x, g)


def _resid_norm_kernel(h_ref, y_ref, gp_ref, *rest, with_next):
    h_new = h_ref[...] + _rms(y_ref[...].astype(F32), gp_ref[...])
    if with_next:
        gn_ref, ho_ref, hn_ref = rest
        hn_ref[...] = _rms(h_new, gn_ref[...]).astype(hn_ref.dtype)
    else:
        (ho_ref,) = rest
    ho_ref[...] = h_new


def _resid_norm(h, y, g_post, layer, g_next=None, layer_next=None, *, tm=256):
    m, d = h.shape
    tm = min(tm, m)
    row = pl.BlockSpec((tm, d), lambda i: (i, 0))
    with_next = g_next is not None
    in_specs = [row, row, pl.BlockSpec((None, 1, d), lambda i: (layer, 0, 0))]
    args = [h, y, g_post]
    out_shape = [jax.ShapeDtypeStruct((m, d), F32)]
    out_specs = [row]
    if with_next:
        in_specs.append(pl.BlockSpec((None, 1, d), lambda i: (layer_next, 0, 0)))
        args.append(g_next)
        out_shape.append(jax.ShapeDtypeStruct((m, d), BF16))
        out_specs.append(row)
    res = pl.pallas_call(
        functools.partial(_resid_norm_kernel, with_next=with_next),
        grid=(m // tm,),
        in_specs=in_specs,
        out_specs=out_specs,
        out_shape=out_shape,
        compiler_params=_cparams(("parallel",)),
        name="resid_norm",
    )(*args)
    return (res[0], res[1]) if with_next else (res[0], None)


def _lower_bound_kernel(x_ref, llb_ref, l1m_ref):
    depth = x_ref.shape[0]
    rows = [x_ref[i:i + 1, :].astype(F32) for i in range(depth)]
    mx = functools.reduce(jnp.maximum, rows)
    ex = [jnp.exp(r - mx) for r in rows]
    tot = functools.reduce(lambda a, b: a + b, ex)
    p = [e / tot for e in ex]
    cum = p[0]
    for i in range(depth):
        if i > 0:
            cum = cum + p[i]
        lb = jnp.clip(cum - p[0], 0.0, 1.0 - 1e-6)
        llb_ref[i:i + 1, :] = jnp.log(jnp.maximum(lb, LB_FLOOR))
        l1m_ref[i:i + 1, :] = jnp.log1p(-lb)


def _lower_bounds(logits):
    shp = jax.ShapeDtypeStruct(logits.shape, F32)
    return pl.pallas_call(_lower_bound_kernel, out_shape=[shp, shp], name="hgrn_lower_bounds")(logits)


def _hgrn_kernel(q_ref, v_ref, gate_ref, z_ref, llb_ref, l1m_ref, g_ref, o_ref,
                 state_ref, b_scr, k_scr, v_scr, *, n_chunks, unroll):
    c_len = HG_CHUNK
    n_rows = c_len // SUBLANES

    @pl.when(pl.program_id(2) == 0)
    def _():
        state_ref[...] = jnp.zeros_like(state_ref)

    lbf = jnp.exp(llb_ref[...])
    oml = jnp.exp(l1m_ref[...])
    gain = g_ref[...]
    r_i = lax.broadcasted_iota(jnp.int32, (c_len, c_len), 0)
    c_i = lax.broadcasted_iota(jnp.int32, (c_len, c_len), 1)
    tri = (c_i <= r_i).astype(F32)
    sub = lax.broadcasted_iota(jnp.int32, (SUBLANES, LANES), 0)
    takes = [(c_i >= jb * HG_SUB) & (c_i < (jb + 1) * HG_SUB) & (r_i >= (jb + 1) * HG_SUB)
             for jb in range(c_len // HG_SUB - 1)]

    def chunk(c, slot, state_t):
        rows = pl.ds(pl.multiple_of(c * c_len, c_len), c_len)
        z = z_ref[rows, :]
        q = q_ref[rows, :].astype(F32)
        v16 = v_ref[rows, :]
        e = jnp.exp(-jnp.abs(z))
        r = 1.0 / (1.0 + e)
        er = e * r
        pos = z >= 0.0
        log_f = jnp.log(lbf + oml * jnp.where(pos, r, er))
        kk = oml * jnp.where(pos, er, r)
        b = jnp.dot(tri, log_f, preferred_element_type=F32, precision=lax.Precision.HIGHEST) * LOG2E
        b_scr[slot] = b
        k_scr[slot] = kk
        v_scr[slot] = v16.astype(F32)

        q_rows = [q[r * SUBLANES:(r + 1) * SUBLANES, :] for r in range(n_rows)]
        b_rows = [b[r * SUBLANES:(r + 1) * SUBLANES, :] for r in range(n_rows)]
        o_rows = [jnp.zeros((SUBLANES, LANES), F32) for _ in range(n_rows)]
        rows_per_sb = HG_SUB // SUBLANES
        for g in range(n_rows):
            r_hi = (g // rows_per_sb + 1) * rows_per_sb
            for i in range(SUBLANES):
                s = g * SUBLANES + i
                bs = b_scr[slot, s:s + 1, :]
                ks = k_scr[slot, s:s + 1, :]
                vs = v_scr[slot, s:s + 1, :]
                for r in range(g, r_hi):
                    e = jnp.exp2(b_rows[r] - bs)
                    if r == g and i > 0:
                        e = jnp.where(sub >= i, e, 0.0)
                    p = e * (q_rows[r] * ks)
                    o_rows[r] = o_rows[r] + jnp.sum(p, axis=-1, keepdims=True) * vs
        o = jnp.concatenate(o_rows, axis=0)

        n_sb = c_len // HG_SUB
        bend = [b[(jb + 1) * HG_SUB - 1:(jb + 1) * HG_SUB, :] for jb in range(n_sb)]
        bend_rows = jnp.concatenate([jnp.broadcast_to(e_, (HG_SUB, LANES)) for e_ in bend], axis=0)
        khat = (kk * jnp.exp2(bend_rows - b)).astype(BF16)
        a_stack = jnp.concatenate(
            [q * jnp.exp2(jnp.minimum(b - bend[jb], 0.0)) for jb in range(n_sb - 1)], axis=0).astype(BF16)
        r_all = lax.dot_general(a_stack, khat, (((1,), (1,)), ((), ())), preferred_element_type=F32)
        s_off = jnp.zeros((c_len, c_len), F32)
        for jb in range(n_sb - 1):
            s_off = jnp.where(takes[jb], r_all[jb * c_len:(jb + 1) * c_len, :], s_off)
        o = o + jnp.dot(s_off.astype(BF16), v16, preferred_element_type=F32)

        qe = (q * jnp.exp2(b)).astype(BF16)
        o = o + lax.dot_general(qe, state_t.astype(BF16), (((1,), (1,)), ((), ())),
                                preferred_element_type=F32)
        b_end = b[c_len - 1:c_len, :]
        kd = (kk * jnp.exp2(b_end - b)).astype(BF16)
        upd = lax.dot_general(v16, kd, (((0,), (0,)), ((), ())), preferred_element_type=F32)
        state_t = state_t * jnp.exp2(b_end) + upd

        gate = gate_ref[rows, :].astype(F32)
        y = _rms(o, gain) * (gate * jax.nn.sigmoid(gate))
        o_ref[rows, :] = y.astype(o_ref.dtype)
        return state_t

    def trip(cc, state_t):
        for slot in range(unroll):
            state_t = chunk(unroll * cc + slot, slot, state_t)
        return state_t

    state_ref[...] = lax.fori_loop(0, n_chunks // unroll, trip, state_ref[...])


def _hgrn(proj16, proj32, llb, l1m, mix_g, layer, *, batch, seq, blk=HG_BLOCK):
    blk = min(blk, seq)
    ns = seq // blk
    unroll = min(HG_UNROLL, blk // HG_CHUNK)
    assert (blk // HG_CHUNK) % unroll == 0
    tok = lambda col0: pl.BlockSpec((blk, LANES), lambda b, h, s: (b * ns + s, col0 + h))
    par = pl.BlockSpec((None, 1, LANES), lambda b, h, s: (layer, 0, h))
    return pl.pallas_call(
        functools.partial(_hgrn_kernel, n_chunks=blk // HG_CHUNK, unroll=unroll),
        grid=(batch, HG_HEADS, ns),
        in_specs=[tok(P16_HQ), tok(P16_HI), tok(P16_HG), tok(P32_HF), par, par, par],
        out_specs=pl.BlockSpec((blk, LANES), lambda b, h, s: (b * ns + s, h)),
        out_shape=jax.ShapeDtypeStruct((batch * seq, HG_HEADS * HG_DV), BF16),
        scratch_shapes=[pltpu.VMEM((HG_DV, HG_DK), F32)] + [pltpu.VMEM((unroll, HG_CHUNK, LANES), F32)] * 3,
        compiler_params=_cparams(("parallel", "parallel", "arbitrary")),
        name="hgrn2_scan",
    )(proj16, proj16, proj16, proj32, llb, l1m, mix_g)


def _fox_prep_kernel(x_ref, bias_ref, fq_ref, fk_ref, fv_ref, qo_ref, ko_ref, vo_ref, carry_ref, *, blk):
    @pl.when(pl.program_id(1) == 0)
    def _():
        carry_ref[...] = jnp.zeros_like(carry_ref)

    x = x_ref[...] + bias_ref[...]
    lf = jnp.minimum(x, 0.0) - jnp.log1p(jnp.exp(-jnp.abs(x)))
    r_i = lax.broadcasted_iota(jnp.int32, (blk, blk), 0)
    c_i = lax.broadcasted_iota(jnp.int32, (blk, blk), 1)
    tri = (c_i <= r_i).astype(F32)
    cum = jnp.dot(tri, lf, preferred_element_type=F32, precision=lax.Precision.HIGHEST) + carry_ref[...]
    carry_ref[...] = cum[blk - 1:blk, :]
    c2 = cum * LOG2E
    lane = lax.broadcasted_iota(jnp.int32, (blk, LANES), 1)
    ones = jnp.ones((blk, LANES), BF16)
    for h in range(FOX_HEADS):
        col = jnp.broadcast_to(c2[:, FF_LANE + h:FF_LANE + h + 1], (blk, LANES))
        hi = col.astype(BF16).astype(F32)
        r1 = col - hi
        mid = r1.astype(BF16).astype(F32)
        lo = r1 - mid
        pieces = jnp.where((lane == 0) | (lane == 3), hi, jnp.where((lane == 1) | (lane == 4), mid, lo))
        q_ext = jnp.where(lane < 3, pieces, jnp.where(lane < 6, 1.0, 0.0))
        k_ext = jnp.where(lane < 3, 1.0, jnp.where(lane < 6, -pieces, 0.0))
        sl = slice(h * LANES, (h + 1) * LANES)
        qo_ref[h, :, :LANES] = fq_ref[:, sl]
        qo_ref[h, :, LANES:] = q_ext.astype(BF16)
        ko_ref[h, :, :LANES] = fk_ref[:, sl]
        ko_ref[h, :, LANES:] = k_ext.astype(BF16)
        vo_ref[h, :, :LANES] = fv_ref[:, sl]
        vo_ref[h, :, LANES:] = ones


def _fox_prep(proj16, proj32, bias_row, layer, *, batch, seq, blk=256):
    blk = min(blk, seq)
    nb = seq // blk
    t = batch * seq
    w = FOX_HEADS * FOX_DIM
    tok = lambda c: pl.BlockSpec((blk, w), lambda b, j: (b * nb + j, P16_FQKV + c))
    out = pl.BlockSpec((FOX_HEADS, blk, ATT_W), lambda b, j: (0, b * nb + j, 0))
    shp = jax.ShapeDtypeStruct((FOX_HEADS, t, ATT_W), BF16)
    return pl.pallas_call(
        functools.partial(_fox_prep_kernel, blk=blk),
        grid=(batch, nb),
        in_specs=[pl.BlockSpec((blk, LANES), lambda b, j: (b * nb + j, 0)),
                  pl.BlockSpec((None, 1, LANES), lambda b, j: (layer, 0, 0)),
                  tok(0), tok(1), tok(2)],
        out_specs=[out, out, out],
        out_shape=[shp, shp, shp],
        scratch_shapes=[pltpu.VMEM((1, LANES), F32)],
        compiler_params=_cparams(("parallel", "arbitrary")),
        name="fox_prep",
    )(proj32, bias_row, proj16, proj16, proj16)


def _flash_kernel(q_ref, k_ref, v_ref, g_ref, o_ref, m_scr, acc_scr, *, tq, sub, hp):
    i = pl.program_id(2)
    n_sub = tq // sub
    m_scr[...] = jnp.full_like(m_scr, MASK_VALUE)
    acc_scr[...] = jnp.zeros_like(acc_scr)
    r_i = lax.broadcasted_iota(jnp.int32, (sub, sub), 0)
    c_i = lax.broadcasted_iota(jnp.int32, (sub, sub), 1)

    def step(hd, r, kv0, n_keys, diagonal):
        rows = slice(r * sub, (r + 1) * sub)
        keys = pl.ds(kv0, n_keys)
        s = lax.dot_general(q_ref[hd, rows, :], k_ref[hd, keys, :], (((1,), (1,)), ((), ())),
                            preferred_element_type=F32)
        if diagonal:
            s = jnp.where(c_i <= r_i, s, MASK_VALUE)
        m_prev = m_scr[hd, rows, :]
        m_new = jnp.maximum(m_prev, jnp.max(s, axis=-1, keepdims=True))
        alpha = jnp.exp2(m_prev - m_new)
        p = jnp.exp2(s - jnp.concatenate([m_new] * (n_keys // LANES), axis=1)).astype(BF16)
        pv = jnp.dot(p, v_ref[hd, keys, :], preferred_element_type=F32)
        acc_scr[hd, rows, :] = jnp.concatenate([alpha, alpha], axis=1) * acc_scr[hd, rows, :] + pv
        m_scr[hd, rows, :] = m_new

    wide = FLASH_WIDE * sub

    def full_chunks(j, carry):
        kv0 = pl.multiple_of(j * wide, wide)
        for hd in range(hp):
            for r in range(n_sub):
                step(hd, r, kv0, wide, False)
        return carry

    assert n_sub % FLASH_WIDE == 0
    lax.fori_loop(0, i * n_sub // FLASH_WIDE, full_chunks, 0)
    base = pl.multiple_of(i * tq, tq)
    for jj in range(n_sub):
        for hd in range(hp):
            for r in range(jj, n_sub):
                step(hd, r, base + jj * sub, sub, jj == r)

    for hd in range(hp):
        acc = acc_scr[hd]
        o = acc[:, :LANES] / acc[:, LANES:]
        cols = slice(hd * LANES, (hd + 1) * LANES)
        o_ref[:, cols] = _rms(o, g_ref[:, cols]).astype(o_ref.dtype)


def _flash(q, k, v, gain, layer, g_off, *, batch, seq, name):
    heads, t, _ = q.shape
    tq, sub, hp = min(FLASH_TQ, seq), min(FLASH_SUB, seq), FLASH_HP
    nq = seq // tq
    assert heads % hp == 0 and g_off % hp == 0
    kv = pl.BlockSpec((hp, seq, ATT_W), lambda b, h, i: (h, b, 0))
    return pl.pallas_call(
        functools.partial(_flash_kernel, tq=tq, sub=sub, hp=hp),
        grid=(batch, heads // hp, nq),
        in_specs=[pl.BlockSpec((hp, tq, ATT_W), lambda b, h, i: (h, b * nq + i, 0)), kv, kv,
                  pl.BlockSpec((None, 1, hp * LANES), lambda b, h, i: (layer, 0, g_off // hp + h))],
        out_specs=pl.BlockSpec((tq, hp * LANES), lambda b, h, i: (b * nq + i, h)),
        out_shape=jax.ShapeDtypeStruct((t, heads * LANES), BF16),
        scratch_shapes=[pltpu.VMEM((hp, tq, LANES), F32), pltpu.VMEM((hp, tq, ATT_W), F32)],
        compiler_params=_cparams(("parallel", "parallel", "arbitrary"), VMEM_BIG_MB),
        name=name,
    )(q, k, v, gain)


def _rope128(x, tab):
    c, s1, s2 = tab[:, :LANES], tab[:, LANES:2 * LANES], tab[:, 2 * LANES:]
    half = MLA_ROPE // 2
    return x * c + pltpu.roll(x, LANES - half, axis=1) * s1 + pltpu.roll(x, half, axis=1) * s2


def _mla_prep_kernel(tail_ref, ang_ref, gq_ref, gkv_ref, cqn_ref, ckvn_ref, krot_ref, tab_ref):
    tail = pltpu.roll(tail_ref[...], TAIL_W - TAIL_SHIFT, axis=1)
    cq = tail[:, :MLA_Q_LORA]
    ckv = tail[:, MLA_Q_LORA:MLA_Q_LORA + MLA_KV_LORA]
    misc = tail[:, MLA_Q_LORA + MLA_KV_LORA:]
    cqn_ref[...] = _rms(cq, gq_ref[...]).astype(cqn_ref.dtype)
    ckvn_ref[...] = _rms(ckv, gkv_ref[...]).astype(ckvn_ref.dtype)
    ang = ang_ref[...]
    cos, sin = jnp.cos(ang), jnp.sin(ang)
    lane = lax.broadcasted_iota(jnp.int32, ang.shape, 1)
    half = MLA_ROPE // 2
    tab = jnp.concatenate([
        jnp.where(lane < MLA_ROPE, cos, 0.0),
        jnp.where(lane < half, -sin, 0.0),
        jnp.where((lane >= half) & (lane < MLA_ROPE), sin, 0.0)], axis=1)
    tab_ref[...] = tab
    krot_ref[...] = _rope128(misc, tab).astype(krot_ref.dtype)


def _mla_prep(proj32, ang, gq, gkv, layer, *, tm=512):
    t = proj32.shape[0]
    tm = min(tm, t)
    return pl.pallas_call(
        _mla_prep_kernel,
        grid=(t // tm,),
        in_specs=[pl.BlockSpec((tm, TAIL_W), lambda i: (i, 0)),
                  pl.BlockSpec((tm, LANES), lambda i: (i, 0)),
                  pl.BlockSpec((None, 1, MLA_Q_LORA), lambda i: (layer, 0, 0)),
                  pl.BlockSpec((None, 1, MLA_KV_LORA), lambda i: (layer, 0, 0))],
        out_specs=[pl.BlockSpec((tm, MLA_Q_LORA), lambda i: (i, 0)),
                   pl.BlockSpec((tm, MLA_KV_LORA), lambda i: (i, 0)),
                   pl.BlockSpec((tm, LANES), lambda i: (i, 0)),
                   pl.BlockSpec((tm, 3 * LANES), lambda i: (i, 0))],
        out_shape=[jax.ShapeDtypeStruct((t, MLA_Q_LORA), BF16),
                   jax.ShapeDtypeStruct((t, MLA_KV_LORA), BF16),
                   jax.ShapeDtypeStruct((t, LANES), BF16),
                   jax.ShapeDtypeStruct((t, 3 * LANES), F32)],
        compiler_params=_cparams(("parallel",)),
        name="mla_prep",
    )(proj32, ang, gq, gkv)


def _mla_proj_kernel(cqn_ref, ckvn_ref, krot_ref, tab_ref, wq_ref, wkv_ref, q_ref, k_ref, v_ref, *, hpt):
    q = jnp.dot(cqn_ref[...], wq_ref[...], preferred_element_type=F32)
    kv = jnp.dot(ckvn_ref[...], wkv_ref[...], preferred_element_type=F32)
    tab, krot = tab_ref[...], krot_ref[...]
    ones = jnp.ones((v_ref.shape[1], LANES), v_ref.dtype)
    for h in range(hpt):
        c = h * ATT_W
        q_ref[h, :, :LANES] = q[:, c:c + LANES].astype(q_ref.dtype)
        q_ref[h, :, LANES:] = _rope128(q[:, c + LANES:c + ATT_W], tab).astype(q_ref.dtype)
        k_ref[h, :, :LANES] = kv[:, c:c + LANES].astype(k_ref.dtype)
        k_ref[h, :, LANES:] = krot
        v_ref[h, :, :LANES] = kv[:, c + LANES:c + ATT_W].astype(v_ref.dtype)
        v_ref[h, :, LANES:] = ones


def _mla_proj(cqn, ckvn, krot, tab, wq, wkv, layer, *, tm=1024, hpt=4):
    t = cqn.shape[0]
    tm = min(tm, t)
    tok = lambda w: pl.BlockSpec((tm, w), lambda i, j: (i, 0))
    out = pl.BlockSpec((hpt, tm, ATT_W), lambda i, j: (j, i, 0))
    shp = jax.ShapeDtypeStruct((MLA_HEADS, t, ATT_W), BF16)
    return pl.pallas_call(
        functools.partial(_mla_proj_kernel, hpt=hpt),
        grid=(t // tm, MLA_HEADS // hpt),
        in_specs=[tok(MLA_Q_LORA), tok(MLA_KV_LORA), tok(LANES), tok(3 * LANES),
                  pl.BlockSpec((None, MLA_Q_LORA, hpt * ATT_W), lambda i, j: (layer, 0, j)),
                  pl.BlockSpec((None, MLA_KV_LORA, hpt * ATT_W), lambda i, j: (layer, 0, j))],
        out_specs=[out, out, out],
        out_shape=[shp, shp, shp],
        compiler_params=_cparams(("parallel", "parallel")),
        name="mla_proj",
    )(cqn, ckvn, krot, tab, wq, wkv)


def _mem_block_kernel(y_ref, h_ref, g0p_ref, g0n_ref, wq_ref, k_ref, v_ref, wo_ref, gp_ref, gn_ref,
                      ho_ref, hno_ref):
    scale = MEM_DIM ** -0.5
    h_mid = h_ref[...] + _rms(y_ref[...].astype(F32), g0p_ref[...])
    hn = _rms(h_mid, g0n_ref[...]).astype(BF16)
    q = jnp.dot(hn, wq_ref[...], preferred_element_type=F32).astype(BF16)
    outs = []
    for h in range(MEM_HEADS):
        sl = slice(h * MEM_DIM, (h + 1) * MEM_DIM)
        s = lax.dot_general(q[:, sl], k_ref[:, sl], (((1,), (1,)), ((), ())),
                            preferred_element_type=F32) * scale
        p = jnp.exp(s - jnp.max(s, axis=-1, keepdims=True))
        p = p / jnp.sum(p, axis=-1, keepdims=True)
        outs.append(jnp.dot(p.astype(BF16), v_ref[:, sl], preferred_element_type=F32))
    o = jnp.concatenate(outs, axis=1).astype(BF16)
    y = jnp.dot(o, wo_ref[...], preferred_element_type=F32)
    h_new = h_mid + _rms(y, gp_ref[...])
    ho_ref[...] = h_new
    hno_ref[...] = _rms(h_new, gn_ref[...]).astype(hno_ref.dtype)


def _mem_block(y, h, g0_post, g0_next, w_mq, km, vm, w_mo, g_post, g_next, layer, *, seq, mem_tokens, tm=256):
    t, d = h.shape
    tm = min(tm, seq)
    nt = seq // tm
    w = MEM_HEADS * MEM_DIM
    row = pl.BlockSpec((tm, d), lambda i: (i, 0))
    kv = pl.BlockSpec((mem_tokens, w), lambda i: (i // nt, 0))
    gain = pl.BlockSpec((None, 1, d), lambda i: (layer, 0, 0))
    return pl.pallas_call(
        _mem_block_kernel,
        grid=(t // tm,),
        in_specs=[row, row, gain, gain, pl.BlockSpec((None, d, w), lambda i: (layer, 0, 0)), kv, kv,
                  pl.BlockSpec((None, w, d), lambda i: (layer, 0, 0)), gain, gain],
        out_specs=[row, row],
        out_shape=[jax.ShapeDtypeStruct((t, d), F32), jax.ShapeDtypeStruct((t, d), BF16)],
        compiler_params=_cparams(("parallel",), VMEM_BIG_MB),
        name="mem_block",
    )(y, h, g0_post, g0_next, w_mq, km, vm, w_mo, g_post, g_next)


def _split_w_in_kernel(w_ref, w16_ref, w32_ref):
    cast = lambda r0, r1: w_ref[r0:r1, :].astype(BF16)
    w16_ref[:1024, :] = cast(0, 1024)
    w16_ref[1024:3072, :] = cast(2048, 4096)
    w16_ref[3072:4096, :] = (w_ref[4096:5120, :] * (FOX_DIM ** -0.5 * LOG2E)).astype(BF16)
    w16_ref[4096:, :] = cast(5120, IN_TAIL0)
    n_tail, tc = w_ref.shape[0] - IN_TAIL0, w_ref.shape[1]
    tail = jnp.concatenate([w_ref[IN_TAIL0:, :], jnp.zeros((TAIL_W - n_tail, tc), F32)], axis=0)
    w32_ref[:TAIL_W, :] = tail.astype(BF16)
    w32_ref[TAIL_W:TAIL_W + 1024, :] = cast(1024, 2048)
    w32_ref[TAIL_W + 1024:, :] = jnp.zeros((P32_COLS - TAIL_W - 1024, tc), BF16)


def _split_w_in(w_in, *, tc=256):
    w_t = jnp.swapaxes(w_in, 1, 2)
    depth, cols, d = w_t.shape
    tc = min(tc, d)
    blk = lambda rows: pl.BlockSpec((None, rows, tc), lambda l, c: (l, 0, c))
    return pl.pallas_call(
        _split_w_in_kernel,
        grid=(depth, d // tc),
        in_specs=[blk(cols)],
        out_specs=[blk(P16_COLS), blk(P32_COLS)],
        out_shape=[jax.ShapeDtypeStruct((depth, P16_COLS, d), BF16),
                   jax.ShapeDtypeStruct((depth, P32_COLS, d), BF16)],
        compiler_params=_cparams(("parallel", "parallel"), VMEM_BIG_MB),
        name="w_in_split",
    )(w_t)


def _split_mla_weights(w_uq, w_ukv):
    depth = w_uq.shape[0]
    wq = w_uq.reshape(depth, MLA_Q_LORA, MLA_HEADS, MLA_NOPE + MLA_ROPE)
    wq = wq * ((MLA_NOPE + MLA_ROPE) ** -0.5 * LOG2E)
    wq = jnp.pad(wq, ((0, 0), (0, 0), (0, 0), (0, ATT_W - MLA_NOPE - MLA_ROPE)))
    assert MLA_NOPE + MLA_V == ATT_W
    return wq.reshape(depth, MLA_Q_LORA, MLA_HEADS * ATT_W).astype(BF16), w_ukv.astype(BF16)


def kernel(x, mem, positions, w_in, hg_lb_logits, fox_f_bias, mla_q_norm_g, mla_kv_norm_g, w_uq, w_ukv,
           mix_out_g, w_o, mem_norm_g, w_mq, w_mk, w_mv, w_mo, w_ff1, w_ff2, pre_mix_g, post_mix_g,
           pre_mem_g, post_mem_g, pre_ffn_g, post_ffn_g):
    batch, seq, d = x.shape
    depth = w_in.shape[0]
    t = batch * seq
    mem_tokens = mem.shape[1]

    w16, w32 = _split_w_in(w_in)
    wq, wkv = _split_mla_weights(w_uq, w_ukv)
    w_o16, w_mq16, w_mk16, w_mv16, w_mo16 = (w.astype(BF16) for w in (w_o, w_mq, w_mk, w_mv, w_mo))
    w_ff1_16, w_ff2_16 = w_ff1.astype(BF16), w_ff2.astype(BF16)
    row = lambda g: g.astype(F32).reshape(depth, 1, g.shape[-1])
    pre_mix, post_mix, pre_mem, post_mem, pre_ffn, post_ffn = map(
        row, (pre_mix_g, post_mix_g, pre_mem_g, post_mem_g, pre_ffn_g, post_ffn_g))
    mix_g, mem_g, gq, gkv = map(row, (mix_out_g, mem_norm_g, mla_q_norm_g, mla_kv_norm_g))
    fox_bias = jnp.pad(fox_f_bias.astype(F32), ((0, 0), (FF_LANE, LANES - FF_LANE - FOX_HEADS)))
    fox_bias = fox_bias.reshape(depth, 1, LANES)
    llb, l1m = _lower_bounds(hg_lb_logits)
    llb, l1m = row(llb), row(l1m)

    half = MLA_ROPE // 2
    inv = ROPE_THETA ** (-jnp.arange(half, dtype=F32) / half)
    ang = positions.reshape(t, 1).astype(F32) * inv[None, :]
    ang = jnp.concatenate([ang, ang, jnp.zeros((t, LANES - MLA_ROPE), F32)], axis=1)

    h = x.reshape(t, d)
    mem2 = mem.reshape(batch * mem_tokens, d)
    hn = _rms_cast(h, pre_mix, 0, name="rms_first")

    for l in range(depth):
        proj16 = _matmul(hn, w16, l, BF16, tm=1024, tn=1024, trans_b=True, name="in_proj16")
        proj32 = _matmul(hn, w32, l, F32, tm=1024, tn=P32_COLS // 2, trans_b=True, name="in_proj32")

        mix_a = _hgrn(proj16, proj32, llb, l1m, mix_g, l, batch=batch, seq=seq)

        fq, fk, fv = _fox_prep(proj16, proj32, fox_bias, l, batch=batch, seq=seq)
        mix_b = _flash(fq, fk, fv, mix_g, l, HG_HEADS, batch=batch, seq=seq, name="fox_attn")

        cqn, ckvn, krot, tab = _mla_prep(proj32, ang, gq, gkv, l)
        q_c, k_c, v_c = _mla_proj(cqn, ckvn, krot, tab, wq, wkv, l)
        mix_c = _flash(q_c, k_c, v_c, mix_g, l, HG_HEADS + FOX_HEADS, batch=batch, seq=seq, name="mla_attn")

        y = _matmul_parts([mix_a, mix_b, mix_c], w_o16, l, BF16, tm=1024, tn=1024, name="mix_out")

        mem_n = _rms_cast(mem2, mem_g, l, name="rms_mem")
        km = _matmul(mem_n, w_mk16, l, BF16, tm=512, tn=512, name="mem_k")
        vm = _matmul(mem_n, w_mv16, l, BF16, tm=512, tn=512, name="mem_v")
        h, hn = _mem_block(y, h, post_mix, pre_mem, w_mq16, km, vm, w_mo16, post_mem, pre_ffn, l,
                           seq=seq, mem_tokens=mem_tokens)

        u = _matmul(hn, w_ff1_16, l, BF16, tm=1024, tn=1024, relu2=True, name="ffn_up")
        y = _matmul(u, w_ff2_16, l, BF16, tm=1024, tn=1024, tk=4096, name="ffn_down")
        if l + 1 < depth:
            h, hn = _resid_norm(h, y, post_ffn, l, pre_mix, l + 1)
        else:
            h, _ = _resid_norm(h, y, post_ffn, l)

    return h.reshape(batch, seq, d)
```
